```python
import jax, jax.numpy as jnp
from jax import lax
import numpy as np

D_MODEL = 2048
BATCH = 4
SEQ = 4096
DEPTH = 1

CHUNK = 64
RET_HEADS = 8
RET_DK = D_MODEL // RET_HEADS
RET_DV = D_MODEL // RET_HEADS
HG_EXPAND = 128
HG_HEADS = D_MODEL // HG_EXPAND
HG_DK = HG_EXPAND
HG_DV = D_MODEL // HG_HEADS
N_SPLITS = 10
N_EXPERTS = 32
TOP_K = 4
D_FF = D_MODEL
SWIGLU_ALPHA = 1.702
SWIGLU_LIMIT = 7.0
EXPERT_BLOCK = 128
ROPE_BASE = 10000.0
EPS = 1e-6

kernel_name = "hybrid_retention_hgrn2_moe_block"


def rmsnorm(x, w):
    xf = x.astype(jnp.float32)
    y = xf * lax.rsqrt(jnp.mean(xf * xf, axis=-1, keepdims=True) + EPS) * w.astype(jnp.float32)
    return y.astype(x.dtype)


def head_norm(o, center):
    if center:
        o = o - jnp.mean(o, axis=-1, keepdims=True)
    return o * lax.rsqrt(jnp.mean(o * o, axis=-1, keepdims=True) + EPS)


def rotary(x):
    s, dh = x.shape[1], x.shape[3]
    inv = 1.0 / (ROPE_BASE ** jnp.linspace(0.0, 1.0, dh // 2, dtype=jnp.float32))
    ang = jnp.arange(s, dtype=jnp.float32)[:, None] * inv[None, :]
    cos = jnp.cos(ang)[None, :, None, :]
    sin = jnp.sin(ang)[None, :, None, :]
    x1, x2 = x[..., : dh // 2], x[..., dh // 2:]
    return jnp.concatenate([x1 * cos - x2 * sin, x2 * cos + x1 * sin], axis=-1)


def retention(q, k, v):
    b, s, h, dk = q.shape
    dv = v.shape[-1]
    nc = s // CHUNK
    log_gamma = jnp.log1p(-jnp.exp2(-5.0 - jnp.arange(h, dtype=jnp.float32)))
    q = rotary(q)
    k = rotary(k) * (dk ** -0.5)
    qc = q.reshape(b, nc, CHUNK, h, dk)
    kc = k.reshape(b, nc, CHUNK, h, dk)
    vc = v.reshape(b, nc, CHUNK, h, dv)
    pos = jnp.arange(CHUNK, dtype=jnp.float32)
    dist = jnp.abs(pos[:, None] - pos[None, :])
    decay_in = jnp.exp(dist[None] * log_gamma[:, None, None])
    scores = jnp.einsum('bnqhd,bnkhd->bnhqk', qc, kc) * decay_in
    intra = jnp.einsum('bnhqk,bnkhe->bnqhe', scores, vc)
    xi = jnp.exp((pos[:, None] + 1.0) * log_gamma[None])
    zeta = jnp.exp((CHUNK - 1.0 - pos)[:, None] * log_gamma[None])
    gamma_c = jnp.exp(CHUNK * log_gamma)

    def step(state, inp):
        q_n, k_n, v_n = inp
        out = jnp.einsum('bqhd,bhde->bqhe', q_n * xi[None, :, :, None], state)
        state = gamma_c[None, :, None, None] * state + jnp.einsum(
            'bkhd,bkhe->bhde', k_n * zeta[None, :, :, None], v_n)
        return state, out

    state0 = jnp.zeros((b, h, dk, dv), jnp.float32)
    _, inter = lax.scan(step, state0, (jnp.moveaxis(qc, 1, 0), jnp.moveaxis(kc, 1, 0), jnp.moveaxis(vc, 1, 0)))
    inter = jnp.moveaxis(inter, 0, 1)
    return (intra + inter).reshape(b, s, h, dv)


def hgrn2(q, f_logit, i, lb):
    b, s, h, dk = q.shape
    dv = i.shape[-1]
    nc = s // CHUNK
    q = jax.nn.silu(q)
    log_f = jnp.log(lb + (1.0 - lb) * jax.nn.sigmoid(f_logit))
    k = (1.0 - lb) * jax.nn.sigmoid(-f_logit)
    to_c = lambda a: jnp.moveaxis(a.reshape(b, nc, CHUNK, h, a.shape[-1]), 1, 0)
    pos = jnp.arange(CHUNK)
    causal = (pos[:, None] >= pos[None, :])[None, :, :, None, None]

    def step(state, inp):
        q_n, k_n, v_n, lf_n = inp
        cum = jnp.cumsum(lf_n, axis=1)
        diff = cum[:, :, None] - cum[:, None, :]
        decay = jnp.where(causal, jnp.exp(jnp.minimum(diff, 0.0)), 0.0)
        scores = jnp.einsum('bjmhd,bjhd->bjmh', decay * k_n[:, None], q_n)
        intra = jnp.einsum('bjmh,bmhe->bjhe', scores, v_n)
        inter = jnp.einsum('bjhd,bhde->bjhe', q_n * jnp.exp(cum), state)
        last = cum[:, -1]
        state = jnp.exp(last)[..., None] * state + jnp.einsum(
            'bmhd,bmhe->bhde', k_n * jnp.exp(last[:, None] - cum), v_n)
        return state, intra + inter

    state0 = jnp.zeros((b, h, dk, dv), jnp.float32)
    _, out = lax.scan(step, state0, (to_c(q), to_c(k), to_c(i), to_c(log_f)))
    return jnp.moveaxis(out, 0, 1).reshape(b, s, h, dv)


def mixer(hn, w_in, ret_norm_w, hg_norm_w, lb, w_out):
    b, s, _ = hn.shape
    z = hn @ w_in
    parts = [p.astype(jnp.float32) for p in jnp.split(z, N_SPLITS, axis=-1)]
    rq, rk, rv, rg, hq, hf, hi, hg, gate_r, gate_h = parts
    heads = lambda a, n: a.reshape(b, s, n, a.shape[-1] // n)
    o_r = retention(heads(rq, RET_HEADS), heads(rk, RET_HEADS), heads(rv, RET_HEADS))
    o_r = head_norm(o_r, True).reshape(b, s, D_MODEL) * ret_norm_w.astype(jnp.float32) * jax.nn.silu(rg)
    o_h = hgrn2(heads(hq, HG_HEADS), heads(hf, HG_HEADS), heads(hi, HG_HEADS), lb.reshape(HG_HEADS, HG_DK))
    o_h = head_norm(o_h, False).reshape(b, s, D_MODEL) * hg_norm_w.astype(jnp.float32) * jax.nn.silu(hg)
    merged = jax.nn.sigmoid(gate_r) * o_r + jax.nn.sigmoid(gate_h) * o_h
    return merged.astype(hn.dtype) @ w_out


def clamped_swiglu(hcat):
    x_glu = jnp.minimum(hcat[..., ::2], SWIGLU_LIMIT)
    x_lin = jnp.clip(hcat[..., 1::2], -SWIGLU_LIMIT, SWIGLU_LIMIT)
    return x_glu * jax.nn.sigmoid(SWIGLU_ALPHA * x_glu) * (x_lin + 1.0)


def moe(hn, w_router, b_router, w1, b1, w2, b2):
    b, s, d = hn.shape
    n = b * s
    xt = hn.reshape(n, d)
    logits = (xt @ w_router + b_router).astype(jnp.float32)
    top_vals, top_idx = lax.top_k(logits, TOP_K)
    gates = jax.nn.softmax(top_vals, axis=-1)
    e_flat = top_idx.reshape(-1)
    tok_flat = jnp.repeat(jnp.arange(n, dtype=jnp.int32), TOP_K)
    g_flat = gates.reshape(-1)
    order = jnp.argsort(e_flat)
    e_sorted, tok_sorted, g_sorted = e_flat[order], tok_flat[order], g_flat[order]
    counts = jnp.bincount(e_flat, length=N_EXPERTS)
    start = jnp.cumsum(counts) - counts
    padded = (counts + EXPERT_BLOCK - 1) // EXPERT_BLOCK * EXPERT_BLOCK
    end_pad = jnp.cumsum(padded)
    start_pad = end_pad - padded
    rank = jnp.arange(n * TOP_K, dtype=jnp.int32) - start[e_sorted]
    dest = start_pad[e_sorted] + rank
    m_pad = -(-(n * TOP_K) // EXPERT_BLOCK) * EXPERT_BLOCK + N_EXPERTS * EXPERT_BLOCK
    n_blocks = m_pad // EXPERT_BLOCK
    tok_pad = jnp.zeros((m_pad,), jnp.int32).at[dest].set(tok_sorted)
    g_pad = jnp.zeros((m_pad,), jnp.float32).at[dest].set(g_sorted)
    block_start = jnp.arange(n_blocks, dtype=jnp.int32) * EXPERT_BLOCK
    block_expert = jnp.minimum(jnp.sum(end_pad[None, :] <= block_start[:, None], axis=1), N_EXPERTS - 1)
    x_blocks = xt[tok_pad].reshape(n_blocks, EXPERT_BLOCK, d)

    def expert_block(args):
        xb, e = args
        hid = clamped_swiglu(xb @ w1[e] + b1[e])
        return hid @ w2[e] + b2[e]

    y_blocks = lax.map(expert_block, (x_blocks, block_expert))
    y_rows = y_blocks.reshape(m_pad, d).astype(jnp.float32) * g_pad[:, None]
    out = jax.ops.segment_sum(y_rows, tok_pad, num_segments=n)
    return out.reshape(b, s, d).astype(hn.dtype)


def setup_inputs(seed: int = 0) -> dict:
    key = jax.random.key(seed)
    ks = jax.random.split(key, 16)
    f32 = jnp.float32
    nrm = lambda k, shape, scale: jax.random.normal(k, shape, f32) * scale
    return {
        "x": nrm(ks[0], (BATCH, SEQ, D_MODEL), 1.0),
        "w_in": nrm(ks[1], (DEPTH, D_MODEL, N_SPLITS * D_MODEL), D_MODEL ** -0.5),
        "ret_norm_w": 1.0 + nrm(ks[2], (DEPTH, D_MODEL), 0.1),
        "hg_norm_w": 1.0 + nrm(ks[3], (DEPTH, D_MODEL), 0.1),
        "hg_lb_logits": nrm(ks[4], (DEPTH + 1, HG_HEADS * HG_DK), 0.1),
        "w_out": nrm(ks[5], (DEPTH, D_MODEL, D_MODEL), D_MODEL ** -0.5),
        "norm_mix_w": 1.0 + nrm(ks[6], (DEPTH, D_MODEL), 0.1),
        "norm_moe_w": 1.0 + nrm(ks[7], (DEPTH, D_MODEL), 0.1),
        "w_router": nrm(ks[8], (DEPTH, D_MODEL, N_EXPERTS), D_MODEL ** -0.5),
        "b_router": nrm(ks[9], (DEPTH, N_EXPERTS), 0.01),
        "w1": nrm(ks[10], (DEPTH, N_EXPERTS, D_MODEL, 2 * D_FF), D_MODEL ** -0.5),
        "b1": nrm(ks[11], (DEPTH, N_EXPERTS, 2 * D_FF), 0.01),
        "w2": nrm(ks[12], (DEPTH, N_EXPERTS, D_FF, D_MODEL), D_FF ** -0.5),
        "b2": nrm(ks[13], (DEPTH, N_EXPERTS, D_MODEL), 0.01),
        "norm_final_w": 1.0 + nrm(ks[14], (D_MODEL,), 0.1),
    }


def reference(x, w_in, ret_norm_w, hg_norm_w, hg_lb_logits, w_out, norm_mix_w, norm_moe_w,
              w_router, b_router, w1, b1, w2, b2, norm_final_w):
    lb_all = jnp.cumsum(jax.nn.softmax(hg_lb_logits.astype(jnp.float32), axis=0), axis=0)
    h = x
    for l in range(DEPTH):
        h = h + mixer(rmsnorm(h, norm_mix_w[l]), w_in[l], ret_norm_w[l], hg_norm_w[l], lb_all[l], w_out[l])
        h = h + moe(rmsnorm(h, norm_moe_w[l]), w_router[l], b_router[l], w1[l], b1[l], w2[l], b2[l])
    return rmsnorm(h, norm_final_w)
```

```python
import functools

import numpy as np
import jax
import jax.numpy as jnp
from jax import lax
from jax.experimental import pallas as pl
from jax.experimental.pallas import tpu as pltpu

F32 = jnp.float32
BF16 = jnp.bfloat16

LANES = 128
CHUNK = 64
RET_HEADS = 8
HG_HEADS = 16
N_SPLITS = 10
N_EXPERTS = 32
TOP_K = 4
SWIGLU_ALPHA = 1.702
SWIGLU_LIMIT = 7.0
ROPE_BASE = 10000.0
EPS = 1e-6
VMEM_LIMIT = 56 * 1024 * 1024

P_RQ, P_RK, P_RV, P_RG, P_HQ, P_HF, P_HI, P_HG, P_GR, P_GH = range(N_SPLITS)


def _sigmoid(x):
    return 1.0 / (1.0 + jnp.exp(-x))


def _silu(x):
    return x * _sigmoid(x)


def _dot(a, b):
    return jnp.dot(a, b, preferred_element_type=F32)


def _dot_nt(a, b):
    return lax.dot_general(a, b, (((1,), (1,)), ((), ())), preferred_element_type=F32)


def _dot_tn(a, b):
    return lax.dot_general(a, b, (((0,), (0,)), ((), ())), preferred_element_type=F32)


def _inproj_kernel(x_ref, nw_ref, w_ref, o_ref, xn_ref):
    @pl.when(pl.program_id(1) == 0)
    def _():
        x = x_ref[...]
        ms = jnp.mean(x * x, axis=-1, keepdims=True)
        xn_ref[...] = (x * lax.rsqrt(ms + EPS) * nw_ref[...]).astype(BF16)

    acc = _dot(xn_ref[...], w_ref[...])
    for g in range(o_ref.shape[0]):
        o_ref[g] = acc[:, g * LANES:(g + 1) * LANES].astype(o_ref.dtype)


def _inproj(x2, norm_w, w_bf, tm=1024, tn=1024):
    n, d = x2.shape
    cols = w_bf.shape[1]
    gpt = tn // LANES
    return pl.pallas_call(
        _inproj_kernel,
        out_shape=jax.ShapeDtypeStruct((cols // LANES, n, LANES), BF16),
        grid=(n // tm, cols // tn),
        in_specs=[
            pl.BlockSpec((tm, d), lambda i, j: (i, 0)),
            pl.BlockSpec((1, d), lambda i, j: (0, 0)),
            pl.BlockSpec((d, tn), lambda i, j: (0, j)),
        ],
        out_specs=pl.BlockSpec((gpt, tm, LANES), lambda i, j: (j, i, 0)),
        scratch_shapes=[pltpu.VMEM((tm, d), BF16)],
        compiler_params=pltpu.CompilerParams(
            dimension_semantics=("parallel", "arbitrary"), vmem_limit_bytes=VMEM_LIMIT),
        name="inproj",
    )(x2, norm_w.reshape(1, d), w_bf)


def _retention_tables(seq, t_blk, dk):
    h = RET_HEADS
    log_gamma = jnp.log1p(-jnp.exp2(-5.0 - jnp.arange(h, dtype=F32)))
    inv = 1.0 / (ROPE_BASE ** jnp.linspace(0.0, 1.0, dk // 2, dtype=F32))
    ang = jnp.arange(seq, dtype=F32)[:, None] * inv[None, :]
    cos, sin = jnp.cos(ang), jnp.sin(ang)
    pos = np.arange(t_blk)
    cq, ck = pos[:, None] // CHUNK, pos[None, :] // CHUNK
    dist = (pos[:, None] - pos[None, :]).astype(np.float32)
    expo = np.where(cq == ck, np.abs(dist), dist)
    allowed = jnp.asarray(cq >= ck)
    dmask = jnp.where(allowed[None], jnp.exp(jnp.asarray(expo)[None] * log_gamma[:, None, None]), 0.0)
    posf = jnp.arange(t_blk, dtype=F32)
    qdec = jnp.exp((posf[None, :] + 1.0) * log_gamma[:, None])[..., None]
    kdec = jnp.exp((t_blk - 1.0 - posf)[None, :] * log_gamma[:, None])[..., None]
    sdec = jnp.broadcast_to(jnp.exp(t_blk * log_gamma)[:, None, None], (h, 1, LANES))
    return cos, sin, dmask, qdec, kdec, sdec


def _retention_kernel(q_ref, k_ref, v_ref, g_ref, gm_ref, cos_ref, sin_ref, dm_ref, qd_ref, kd_ref,
                      sd_ref, nw_ref, o_ref, st_ref, *, scale):
    @pl.when(pl.program_id(2) == 0)
    def _():
        st_ref[...] = jnp.zeros_like(st_ref)

    cos, sin = cos_ref[...], sin_ref[...]

    def rot(ref):
        x1, x2 = ref[0].astype(F32), ref[1].astype(F32)
        return jnp.concatenate([x1 * cos - x2 * sin, x2 * cos + x1 * sin], axis=-1)

    q = rot(q_ref)
    k = rot(k_ref) * scale
    v = jnp.concatenate([v_ref[0], v_ref[1]], axis=-1)
    q_bf, k_bf = q.astype(BF16), k.astype(BF16)
    scores = _dot_nt(q_bf, k_bf) * dm_ref[...]
    state = st_ref[...]
    o = _dot(scores.astype(BF16), v) + _dot((q * qd_ref[...]).astype(BF16), state.astype(BF16))
    st_ref[...] = sd_ref[:, 0:1] * state + _dot_tn((k * kd_ref[...]).astype(BF16), v)

    o = o - jnp.mean(o, axis=-1, keepdims=True)
    o = o * lax.rsqrt(jnp.mean(o * o, axis=-1, keepdims=True) + EPS)
    gate = jnp.concatenate([g_ref[0], g_ref[1]], axis=-1).astype(F32)
    merge = jnp.concatenate([gm_ref[0], gm_ref[1]], axis=-1).astype(F32)
    o_ref[...] = (_sigmoid(merge) * (o * nw_ref[...] * _silu(gate))).astype(o_ref.dtype)


def _retention(z, ret_norm_w, batch, seq, t_blk=512):
    n = batch * seq
    d = ret_norm_w.shape[0]
    dk = d // RET_HEADS
    gph = dk // LANES
    nt = seq // t_blk
    cos, sin, dmask, qdec, kdec, sdec = _retention_tables(seq, t_blk, dk)

    def part(p):
        return pl.BlockSpec((gph, t_blk, LANES), lambda b, h, t, p=p: (p * RET_HEADS + h, b * nt + t, 0))

    per_head = lambda shape: pl.BlockSpec((None,) + shape, lambda b, h, t: (h, 0, 0))
    return pl.pallas_call(
        functools.partial(_retention_kernel, scale=dk ** -0.5),
        out_shape=jax.ShapeDtypeStruct((n, d), BF16),
        grid=(batch, RET_HEADS, nt),
        in_specs=[
            part(P_RQ), part(P_RK), part(P_RV), part(P_RG), part(P_GR),
            pl.BlockSpec((t_blk, dk // 2), lambda b, h, t: (t, 0)),
            pl.BlockSpec((t_blk, dk // 2), lambda b, h, t: (t, 0)),
            per_head((t_blk, t_blk)), per_head((t_blk, 1)), per_head((t_blk, 1)), per_head((1, LANES)),
            pl.BlockSpec((1, dk), lambda b, h, t: (0, h)),
        ],
        out_specs=pl.BlockSpec((t_blk, dk), lambda b, h, t: (b * nt + t, h)),
        scratch_shapes=[pltpu.VMEM((dk, dk), F32)],
        compiler_params=pltpu.CompilerParams(
            dimension_semantics=("parallel", "parallel", "arbitrary"), vmem_limit_bytes=VMEM_LIMIT),
        name="retention",
    )(z, z, z, z, z, cos, sin, dmask, qdec, kdec, sdec, ret_norm_w.reshape(1, d))


def _hgrn_level_tables():
    c = CHUNK
    r = np.arange(c)
    mats = [(r[None, :] <= r[:, None]), (r[None, :] > r[:, None])]
    masks = [np.eye(c, dtype=bool)]
    s = c // 2
    while s >= 1:
        blk, within = r // (2 * s), r % (2 * s)
        mid = blk * 2 * s + s - 1
        right = within >= s
        t = r[None, :]
        w = np.where(right[:, None], (t > mid[:, None]) & (t <= r[:, None]),
                     (t > r[:, None]) & (t <= mid[:, None]))
        mats.append(w)
        masks.append((blk[:, None] == blk[None, :]) & right[:, None] & (~right)[None, :])
        s //= 2
    wmat = np.concatenate(mats, axis=0).astype(np.float32)
    mask = np.stack(masks).astype(np.float32)
    return jnp.asarray(wmat, BF16), jnp.asarray(mask, F32)


def _split3(x):
    hi = x.astype(BF16)
    r1 = x - hi.astype(F32)
    mid = r1.astype(BF16)
    lo = (r1 - mid.astype(F32)).astype(BF16)
    return hi, mid, lo


def _hgrn_kernel(q_ref, f_ref, i_ref, g_ref, gm_ref, lb_ref, nw_ref, w_ref, m_ref, o_ref,
                 st_ref, qs_ref, ks_ref, lf_ref, ob_ref):
    @pl.when(pl.program_id(2) == 0)
    def _():
        st_ref[...] = jnp.zeros_like(st_ref)

    c = CHUNK
    t_blk, dk = qs_ref.shape
    n_lvl = m_ref.shape[0]
    lb = lb_ref[...]
    z = f_ref[...].astype(F32)
    qs_ref[...] = _silu(q_ref[...].astype(F32))
    ks_ref[...] = (1.0 - lb) * _sigmoid(-z)
    hi, mid, lo = _split3(jnp.log(lb + (1.0 - lb) * _sigmoid(z)))
    lf_ref[...] = jnp.concatenate([hi, mid, lo], axis=-1)

    def chunk(ci, carry):
        r0 = pl.multiple_of(ci * c, c)
        qc = qs_ref[pl.ds(r0, c), :]
        kc = ks_ref[pl.ds(r0, c), :]
        vc = i_ref[pl.ds(r0, c), :]
        g3 = _dot(w_ref[...], lf_ref[pl.ds(r0, c), :])
        e = jnp.exp(g3[:, :dk] + g3[:, dk:2 * dk] + g3[:, 2 * dk:])
        st = st_ref[...]
        inter = _dot_nt((qc * e[0:c]).astype(BF16), st.astype(BF16))
        s = m_ref[0] * _dot_nt(qc.astype(BF16), kc.astype(BF16))
        for l in range(1, n_lvl):
            el = e[(l + 1) * c:(l + 2) * c]
            s = s + m_ref[l] * _dot_nt((qc * el).astype(BF16), (kc * el).astype(BF16))
        ob_ref[pl.ds(r0, c), :] = _dot(s.astype(BF16), vc) + inter
        st_ref[...] = st * e[c - 1:c] + _dot_tn(vc, (kc * e[c:2 * c]).astype(BF16))
        return carry

    lax.fori_loop(0, t_blk // c, chunk, 0)

    o = ob_ref[...]
    o = o * lax.rsqrt(jnp.mean(o * o, axis=-1, keepdims=True) + EPS)
    o = o * nw_ref[...] * _silu(g_ref[...].astype(F32))
    o_ref[...] = (_sigmoid(gm_ref[...].astype(F32)) * o).astype(o_ref.dtype)


def _hgrn2(z, lb, hg_norm_w, batch, seq, t_blk=512):
    n = batch * seq
    d = hg_norm_w.shape[0]
    dk = d // HG_HEADS
    nt = seq // t_blk
    wmat, mask = _hgrn_level_tables()

    def part(p):
        return pl.BlockSpec((None, t_blk, LANES), lambda b, h, t, p=p: (p * HG_HEADS + h, b * nt + t, 0))

    head_vec = pl.BlockSpec((1, dk), lambda b, h, t: (0, h))
    return pl.pallas_call(
        _hgrn_kernel,
        out_shape=jax.ShapeDtypeStruct((n, d), BF16),
        grid=(batch, HG_HEADS, nt),
        in_specs=[
            part(P_HQ), part(P_HF), part(P_HI), part(P_HG), part(P_GH), head_vec, head_vec,
            pl.BlockSpec(wmat.shape, lambda b, h, t: (0, 0)),
            pl.BlockSpec(mask.shape, lambda b, h, t: (0, 0, 0)),
        ],
        out_specs=pl.BlockSpec((t_blk, dk), lambda b, h, t: (b * nt + t, h)),
        scratch_shapes=[
            pltpu.VMEM((dk, dk), F32), pltpu.VMEM((t_blk, dk), F32), pltpu.VMEM((t_blk, dk), F32),
            pltpu.VMEM((t_blk, 3 * dk), BF16), pltpu.VMEM((t_blk, dk), F32),
        ],
        compiler_params=pltpu.CompilerParams(
            dimension_semantics=("parallel", "parallel", "arbitrary"), vmem_limit_bytes=VMEM_LIMIT),
        name="hgrn2",
    )(z, z, z, z, z, lb.reshape(1, d), hg_norm_w.reshape(1, d), wmat, mask)


def _outproj_kernel(a_ref, b_ref, x_ref, w_ref, nw_ref, wr_ref, br_ref, h_ref, hn_ref, gate_ref, idx_ref):
    merged = (a_ref[...].astype(F32) + b_ref[...].astype(F32)).astype(BF16)
    h = x_ref[...] + _dot(merged, w_ref[...])
    h_ref[...] = h
    hn = h * lax.rsqrt(jnp.mean(h * h, axis=-1, keepdims=True) + EPS) * nw_ref[...]
    hn_ref[...] = hn
    logits = jnp.dot(hn, wr_ref[...], preferred_element_type=F32,
                     precision=lax.Precision.HIGHEST) + br_ref[...]
    n_e = logits.shape[-1]
    lane = lax.broadcasted_iota(jnp.int32, logits.shape, 1).astype(F32)
    vals, idxs = [], []
    for _ in range(TOP_K):
        m = jnp.max(logits, axis=-1, keepdims=True)
        idx = jnp.min(jnp.where(logits == m, lane, float(n_e)), axis=-1, keepdims=True)
        vals.append(m)
        idxs.append(idx)
        logits = jnp.where(lane == idx, -jnp.inf, logits)
    exps = [jnp.exp(v - vals[0]) for v in vals]
    denom = exps[0] + exps[1] + exps[2] + exps[3]
    for k in range(TOP_K):
        gate_ref[:, k:k + 1] = exps[k] / denom
        idx_ref[:, k:k + 1] = idxs[k].astype(jnp.int32)


def _outproj(o_r, o_h, x2, w_bf, norm_w, w_router, b_router, tm=512):
    n, d = x2.shape
    n_e = w_router.shape[1]
    row = lambda w: pl.BlockSpec((tm, w), lambda i: (i, 0))
    full = lambda a: pl.BlockSpec(a.shape, lambda i: (0, 0))
    nw, br = norm_w.reshape(1, d), b_router.reshape(1, n_e)
    return pl.pallas_call(
        _outproj_kernel,
        out_shape=(jax.ShapeDtypeStruct((n, d), F32), jax.ShapeDtypeStruct((n, d), F32),
                   jax.ShapeDtypeStruct((n, TOP_K), F32), jax.ShapeDtypeStruct((n, TOP_K), jnp.int32)),
        grid=(n // tm,),
        in_specs=[row(d), row(d), row(d), full(w_bf), full(nw), full(w_router), full(br)],
        out_specs=(row(d), row(d), row(TOP_K), row(TOP_K)),
        compiler_params=pltpu.CompilerParams(
            dimension_semantics=("parallel",), vmem_limit_bytes=VMEM_LIMIT),
        name="outproj_router",
    )(o_r, o_h, x2, w_bf, nw, w_router, br)


def _rank_kernel(idx_ref, tri_ref, rank_ref, cnt_ref, run_ref):
    @pl.when(pl.program_id(0) == 0)
    def _():
        run_ref[...] = jnp.zeros_like(run_ref)

    idx = idx_ref[...]
    lane = lax.broadcasted_iota(jnp.int32, (idx.shape[0], N_EXPERTS), 1)
    run = run_ref[...]
    for k in range(TOP_K):
        onehot = lane == idx[:, k:k + 1]
        before = _dot(tri_ref[...], onehot.astype(BF16)) + run
        rank_ref[:, k:k + 1] = jnp.sum(jnp.where(onehot, before, 0.0), axis=-1, keepdims=True).astype(jnp.int32)
        run = run + jnp.sum(onehot.astype(F32), axis=0, keepdims=True)
    run_ref[...] = run
    cnt_ref[...] = run.astype(jnp.int32)


def _pair_ranks(idx, blk=512):
    n = idx.shape[0]
    r = np.arange(blk)
    tri = jnp.asarray(r[None, :] < r[:, None], BF16)
    return pl.pallas_call(
        _rank_kernel,
        out_shape=(jax.ShapeDtypeStruct((n, TOP_K), jnp.int32), jax.ShapeDtypeStruct((1, N_EXPERTS), jnp.int32)),
        grid=(n // blk,),
        in_specs=[pl.BlockSpec((blk, TOP_K), lambda i: (i, 0)), pl.BlockSpec((blk, blk), lambda i: (0, 0))],
        out_specs=(pl.BlockSpec((blk, TOP_K), lambda i: (i, 0)), pl.BlockSpec((1, N_EXPERTS), lambda i: (0, 0))),
        scratch_shapes=[pltpu.VMEM((1, N_EXPERTS), F32)],
        compiler_params=pltpu.CompilerParams(dimension_semantics=("arbitrary",)),
        name="pair_ranks",
    )(idx, tri)


def _gather_kernel(tok_ref, nrows_ref, src_ref, dst_ref, sem, *, rows):
    i = pl.program_id(0)
    base = i * rows

    @pl.when(base < nrows_ref[0])
    def _():
        def issue(r, carry):
            pltpu.make_async_copy(src_ref.at[pl.ds(tok_ref[base + r], 1)],
                                  dst_ref.at[pl.ds(base + r, 1)], sem).start()
            return carry

        lax.fori_loop(0, rows, issue, 0)
        pltpu.make_async_copy(src_ref.at[pl.ds(0, rows)], dst_ref.at[pl.ds(base, rows)], sem).wait()


def _gather_rows(src, tok_pad, n_rows_used, rows=512):
    m_pad = tok_pad.shape[0]
    return pl.pallas_call(
        functools.partial(_gather_kernel, rows=rows),
        out_shape=jax.ShapeDtypeStruct((m_pad, src.shape[1]), src.dtype),
        grid_spec=pltpu.PrefetchScalarGridSpec(
            num_scalar_prefetch=2, grid=(m_pad // rows,),
            in_specs=[pl.BlockSpec(memory_space=pl.ANY)],
            out_specs=pl.BlockSpec(memory_space=pl.ANY),
            scratch_shapes=[pltpu.SemaphoreType.DMA]),
        compiler_params=pltpu.CompilerParams(dimension_semantics=("arbitrary",)),
        name="gather_rows",
    )(tok_pad, n_rows_used, src)


def _expert_kernel(be_ref, nb_ref, x_ref, w1g_ref, w1l_ref, b1g_ref, b1l_ref, w2_ref, b2_ref, y_ref, xb_ref):
    i, j = pl.program_id(0), pl.program_id(1)

    @pl.when(i < nb_ref[0])
    def _():
        @pl.when(j == 0)
        def _():
            xb_ref[...] = x_ref[...].astype(BF16)

        xb = xb_ref[...]
        glu = jnp.minimum(_dot(xb, w1g_ref[...]) + b1g_ref[...], SWIGLU_LIMIT)
        lin = jnp.clip(_dot(xb, w1l_ref[...]) + b1l_ref[...], -SWIGLU_LIMIT, SWIGLU_LIMIT)
        hid = glu * _sigmoid(SWIGLU_ALPHA * glu) * (lin + 1.0)
        part = _dot(hid.astype(BF16), w2_ref[...])

        @pl.when(j == 0)
        def _():
            y_ref[...] = part + b2_ref[...]

        @pl.when(j > 0)
        def _():
            y_ref[...] += part


def _experts(xs, block_expert, n_blocks_used, w1g, w1l, b1g, b1l, w2b, b2, tm, tf=512):
    m_pad, d = xs.shape
    f = w1g.shape[2]
    nf = f // tf
    nblk = m_pad // tm

    def blk(i, nb):
        return jnp.minimum(i, nb[0] - 1)

    def ftile(i, j, nb):
        return jnp.where(i < nb[0], j, nf - 1)

    return pl.pallas_call(
        _expert_kernel,
        out_shape=jax.ShapeDtypeStruct((m_pad, d), F32),
        grid_spec=pltpu.PrefetchScalarGridSpec(
            num_scalar_prefetch=2, grid=(nblk, nf),
            in_specs=[
                pl.BlockSpec((tm, d), lambda i, j, be, nb: (blk(i, nb), 0)),
                pl.BlockSpec((None, d, tf), lambda i, j, be, nb: (be[blk(i, nb)], 0, ftile(i, j, nb))),
                pl.BlockSpec((None, d, tf), lambda i, j, be, nb: (be[blk(i, nb)], 0, ftile(i, j, nb))),
                pl.BlockSpec((None, 1, tf), lambda i, j, be, nb: (be[blk(i, nb)], 0, ftile(i, j, nb))),
                pl.BlockSpec((None, 1, tf), lambda i, j, be, nb: (be[blk(i, nb)], 0, ftile(i, j, nb))),
                pl.BlockSpec((None, tf, d), lambda i, j, be, nb: (be[blk(i, nb)], ftile(i, j, nb), 0)),
                pl.BlockSpec((None, 1, d), lambda i, j, be, nb: (be[blk(i, nb)], 0, 0)),
            ],
            out_specs=pl.BlockSpec((tm, d), lambda i, j, be, nb: (blk(i, nb), 0)),
            scratch_shapes=[pltpu.VMEM((tm, d), BF16)]),
        compiler_params=pltpu.CompilerParams(
            dimension_semantics=("arbitrary", "arbitrary"), vmem_limit_bytes=VMEM_LIMIT),
        name="experts",
    )(block_expert, n_blocks_used, xs, w1g, w1l, b1g, b1l, w2b, b2)


def _combine_kernel(dest_ref, h_ref, gate_ref, nw_ref, y_ref, o_ref, buf_ref, sem, *, tc):
    base = pl.program_id(0) * tc * TOP_K

    def issue(t, carry):
        for k in range(TOP_K):
            pltpu.make_async_copy(y_ref.at[pl.ds(dest_ref[base + t * TOP_K + k], 1)],
                                  buf_ref.at[k, pl.ds(t, 1)], sem).start()
        return carry

    lax.fori_loop(0, tc, issue, 0)
    for k in range(TOP_K):
        pltpu.make_async_copy(y_ref.at[pl.ds(0, tc)], buf_ref.at[k], sem).wait()
    gate = gate_ref[...]
    out = h_ref[...]
    for k in range(TOP_K):
        out = out + gate[:, k:k + 1] * buf_ref[k]
    out = out * lax.rsqrt(jnp.mean(out * out, axis=-1, keepdims=True) + EPS) * nw_ref[...]
    o_ref[...] = out


def _combine(dest, h, gates, norm_w, y, tc=128):
    n, d = h.shape
    return pl.pallas_call(
        functools.partial(_combine_kernel, tc=tc),
        out_shape=jax.ShapeDtypeStruct((n, d), F32),
        grid_spec=pltpu.PrefetchScalarGridSpec(
            num_scalar_prefetch=1, grid=(n // tc,),
            in_specs=[
                pl.BlockSpec((tc, d), lambda i, dref: (i, 0)),
                pl.BlockSpec((tc, TOP_K), lambda i, dref: (i, 0)),
                pl.BlockSpec((1, d), lambda i, dref: (0, 0)),
                pl.BlockSpec(memory_space=pl.ANY),
            ],
            out_specs=pl.BlockSpec((tc, d), lambda i, dref: (i, 0)),
            scratch_shapes=[pltpu.VMEM((TOP_K, tc, d), F32), pltpu.SemaphoreType.DMA]),
        compiler_params=pltpu.CompilerParams(
            dimension_semantics=("arbitrary",), vmem_limit_bytes=VMEM_LIMIT),
        name="combine_norm",
    )(dest, h, gates, norm_w.reshape(1, d), y)


def _moe(h, hn, gates, idx, w1, b1, w2, b2, norm_final_w, tm=512, tf=512):
    n, d = h.shape
    n_pairs = n * TOP_K
    e_flat = idx.reshape(n_pairs)
    rank, counts = _pair_ranks(idx, blk=min(512, n))
    counts = counts.reshape(N_EXPERTS)
    padded = (counts + tm - 1) // tm * tm
    end_pad = jnp.cumsum(padded)
    start_pad = end_pad - padded
    dest = start_pad[e_flat] + rank.reshape(n_pairs)
    m_pad = n_pairs + N_EXPERTS * tm
    n_blocks = m_pad // tm
    tok_flat = jnp.arange(n_pairs, dtype=jnp.int32) // TOP_K
    tok_pad = jnp.zeros((m_pad,), jnp.int32).at[dest].set(tok_flat)
    block_start = jnp.arange(n_blocks, dtype=jnp.int32) * tm
    block_expert = jnp.minimum(jnp.sum(end_pad[None, :] <= block_start[:, None], axis=1),
                               N_EXPERTS - 1).astype(jnp.int32)
    rows_used = end_pad[-1:].astype(jnp.int32)
    xs = _gather_rows(hn, tok_pad, rows_used)
    w1g, w1l = w1[:, :, 0::2].astype(BF16), w1[:, :, 1::2].astype(BF16)
    b1g, b1l = b1[:, None, 0::2], b1[:, None, 1::2]
    y = _experts(xs, block_expert, rows_used // tm, w1g, w1l, b1g, b1l, w2.astype(BF16), b2[:, None, :], tm, tf)
    return _combine(dest.astype(jnp.int32), h, gates, norm_final_w, y)


def kernel(x, w_in, ret_norm_w, hg_norm_w, hg_lb_logits, w_out, norm_mix_w, norm_moe_w, w_router, b_router,
           w1, b1, w2, b2, norm_final_w):
    batch, seq, d = x.shape
    depth = w_in.shape[0]
    assert depth == 1, "the final norm is fused into the last layer's combine; one layer supported"
    lb_all = jnp.cumsum(jax.nn.softmax(hg_lb_logits.astype(F32), axis=0), axis=0)
    x2 = x.reshape(batch * seq, d)
    z = _inproj(x2, norm_mix_w[0], w_in[0].astype(BF16))
    o_r = _retention(z, ret_norm_w[0], batch, seq)
    o_h = _hgrn2(z, lb_all[0], hg_norm_w[0], batch, seq)
    h, hn, gates, idx = _outproj(o_r, o_h, x2, w_out[0].astype(BF16), norm_moe_w[0], w_router[0], b_router[0])
    out = _moe(h, hn, gates, idx, w1[0], b1[0], w2[0], b2[0], norm_final_w)
    return out.reshape(batch, seq, d)
```

```python
import functools

import numpy as np
import jax
import jax.numpy as jnp
from jax import lax
from jax.experimental import pallas as pl
from jax.experimental.pallas import tpu as pltpu

F32 = jnp.float32
BF16 = jnp.bfloat16

LANES = 128
CHUNK = 64
RET_HEADS = 8
HG_HEADS = 16
N_SPLITS = 10
N_EXPERTS = 32
TOP_K = 4
SWIGLU_ALPHA = 1.702
SWIGLU_LIMIT = 7.0
ROPE_BASE = 10000.0
EPS = 1e-6
LOG2_E = 1.4426950408889634
VMEM_LIMIT = 56 * 1024 * 1024

P_RQ, P_RK, P_RV, P_RG, P_HQ, P_HF, P_HI, P_HG, P_GR, P_GH = range(N_SPLITS)


def _sigmoid(x):
    return 1.0 / (1.0 + jnp.exp(-x))


def _silu(x):
    return x * _sigmoid(x)


def _dot(a, b):
    return jnp.dot(a, b, preferred_element_type=F32)


def _dot_nt(a, b):
    return lax.dot_general(a, b, (((1,), (1,)), ((), ())), preferred_element_type=F32)


def _dot_tn(a, b):
    return lax.dot_general(a, b, (((0,), (0,)), ((), ())), preferred_element_type=F32)


def _inproj_kernel(x_ref, nw_ref, w_ref, o_ref, xn_ref):
    @pl.when(pl.program_id(1) == 0)
    def _():
        x = x_ref[...]
        ms = jnp.mean(x * x, axis=-1, keepdims=True)
        xn_ref[...] = (x * lax.rsqrt(ms + EPS) * nw_ref[...]).astype(BF16)

    acc = _dot(xn_ref[...], w_ref[...])
    for g in range(o_ref.shape[0]):
        o_ref[g] = acc[:, g * LANES:(g + 1) * LANES].astype(o_ref.dtype)


def _inproj(x2, norm_w, w_bf, tm=1024, tn=1024):
    n, d = x2.shape
    cols = w_bf.shape[1]
    gpt = tn // LANES
    return pl.pallas_call(
        _inproj_kernel,
        out_shape=jax.ShapeDtypeStruct((cols // LANES, n, LANES), BF16),
        grid=(n // tm, cols // tn),
        in_specs=[
            pl.BlockSpec((tm, d), lambda i, j: (i, 0)),
            pl.BlockSpec((1, d), lambda i, j: (0, 0)),
            pl.BlockSpec((d, tn), lambda i, j: (0, j)),
        ],
        out_specs=pl.BlockSpec((gpt, tm, LANES), lambda i, j: (j, i, 0)),
        scratch_shapes=[pltpu.VMEM((tm, d), BF16)],
        compiler_params=pltpu.CompilerParams(
            dimension_semantics=("parallel", "arbitrary"), vmem_limit_bytes=VMEM_LIMIT),
        name="inproj",
    )(x2, norm_w.reshape(1, d), w_bf)


def _retention_tables(seq, t_blk, dk):
    h = RET_HEADS
    log_gamma = jnp.log1p(-jnp.exp2(-5.0 - jnp.arange(h, dtype=F32)))
    inv = 1.0 / (ROPE_BASE ** jnp.linspace(0.0, 1.0, dk // 2, dtype=F32))
    ang = jnp.arange(seq, dtype=F32)[:, None] * inv[None, :]
    cos, sin = jnp.cos(ang), jnp.sin(ang)
    pos = np.arange(t_blk)
    cq, ck = pos[:, None] // CHUNK, pos[None, :] // CHUNK
    dist = (pos[:, None] - pos[None, :]).astype(np.float32)
    expo = np.where(cq == ck, np.abs(dist), dist)
    allowed = jnp.asarray(cq >= ck)
    dmask = jnp.where(allowed[None], jnp.exp(jnp.asarray(expo)[None] * log_gamma[:, None, None]), 0.0)
    posf = jnp.arange(t_blk, dtype=F32)
    qdec = jnp.exp((posf[None, :] + 1.0) * log_gamma[:, None])[..., None]
    kdec = jnp.exp((t_blk - 1.0 - posf)[None, :] * log_gamma[:, None])[..., None]
    sdec = jnp.broadcast_to(jnp.exp(t_blk * log_gamma)[:, None, None], (h, 1, LANES))
    return cos, sin, dmask, qdec, kdec, sdec


def _retention_kernel(q_ref, k_ref, v_ref, g_ref, gm_ref, cos_ref, sin_ref, dm_ref, qd_ref, kd_ref,
                      sd_ref, nw_ref, o_ref, st_ref, *, scale):
    @pl.when(pl.program_id(2) == 0)
    def _():
        st_ref[...] = jnp.zeros_like(st_ref)

    cos, sin = cos_ref[...], sin_ref[...]

    def rot(ref):
        x1, x2 = ref[0].astype(F32), ref[1].astype(F32)
        return jnp.concatenate([x1 * cos - x2 * sin, x2 * cos + x1 * sin], axis=-1)

    q = rot(q_ref)
    k = rot(k_ref) * scale
    v = jnp.concatenate([v_ref[0], v_ref[1]], axis=-1)
    q_bf, k_bf = q.astype(BF16), k.astype(BF16)
    scores = _dot_nt(q_bf, k_bf) * dm_ref[...]
    state = st_ref[...]
    o = _dot(scores.astype(BF16), v) + _dot((q * qd_ref[...]).astype(BF16), state.astype(BF16))
    st_ref[...] = sd_ref[:, 0:1] * state + _dot_tn((k * kd_ref[...]).astype(BF16), v)

    o = o - jnp.mean(o, axis=-1, keepdims=True)
    o = o * lax.rsqrt(jnp.mean(o * o, axis=-1, keepdims=True) + EPS)
    gate = jnp.concatenate([g_ref[0], g_ref[1]], axis=-1).astype(F32)
    merge = jnp.concatenate([gm_ref[0], gm_ref[1]], axis=-1).astype(F32)
    o_ref[...] = (_sigmoid(merge) * (o * nw_ref[...] * _silu(gate))).astype(o_ref.dtype)


def _retention(z, ret_norm_w, batch, seq, t_blk=512):
    n = batch * seq
    d = ret_norm_w.shape[0]
    dk = d // RET_HEADS
    gph = dk // LANES
    nt = seq // t_blk
    cos, sin, dmask, qdec, kdec, sdec = _retention_tables(seq, t_blk, dk)

    def part(p):
        return pl.BlockSpec((gph, t_blk, LANES), lambda b, h, t, p=p: (p * RET_HEADS + h, b * nt + t, 0))

    per_head = lambda shape: pl.BlockSpec((None,) + shape, lambda b, h, t: (h, 0, 0))
    return pl.pallas_call(
        functools.partial(_retention_kernel, scale=dk ** -0.5),
        out_shape=jax.ShapeDtypeStruct((n, d), BF16),
        grid=(batch, RET_HEADS, nt),
        in_specs=[
            part(P_RQ), part(P_RK), part(P_RV), part(P_RG), part(P_GR),
            pl.BlockSpec((t_blk, dk // 2), lambda b, h, t: (t, 0)),
            pl.BlockSpec((t_blk, dk // 2), lambda b, h, t: (t, 0)),
            per_head((t_blk, t_blk)), per_head((t_blk, 1)), per_head((t_blk, 1)), per_head((1, LANES)),
            pl.BlockSpec((1, dk), lambda b, h, t: (0, h)),
        ],
        out_specs=pl.BlockSpec((t_blk, dk), lambda b, h, t: (b * nt + t, h)),
        scratch_shapes=[pltpu.VMEM((dk, dk), F32)],
        compiler_params=pltpu.CompilerParams(
            dimension_semantics=("parallel", "parallel", "arbitrary"), vmem_limit_bytes=VMEM_LIMIT),
        name="retention",
    )(z, z, z, z, z, cos, sin, dmask, qdec, kdec, sdec, ret_norm_w.reshape(1, d))


def _hgrn_level_tables():
    c = CHUNK
    r = np.arange(c)
    mats = [(r[None, :] <= r[:, None]), (r[None, :] > r[:, None])]
    masks = [np.eye(c, dtype=bool)]
    s = c // 2
    while s >= 1:
        blk, within = r // (2 * s), r % (2 * s)
        mid = blk * 2 * s + s - 1
        right = within >= s
        t = r[None, :]
        w = np.where(right[:, None], (t > mid[:, None]) & (t <= r[:, None]),
                     (t > r[:, None]) & (t <= mid[:, None]))
        mats.append(w)
        masks.append((blk[:, None] == blk[None, :]) & right[:, None] & (~right)[None, :])
        s //= 2
    wmat = np.concatenate(mats, axis=0).astype(np.float32)
    mask = np.stack(masks).astype(np.float32)
    return jnp.asarray(wmat, BF16), jnp.asarray(mask, F32)


def _split2(x):
    hi = x.astype(BF16)
    return hi, (x - hi.astype(F32)).astype(BF16)


def _hgrn_kernel(q_ref, f_ref, i_ref, g_ref, gm_ref, lb_ref, nw_ref, w_ref, m_ref, o_ref,
                 st_ref, qs_ref, ks_ref, lf_ref, ob_ref, e_ref, s_ref, kv_ref):
    @pl.when(pl.program_id(2) == 0)
    def _():
        st_ref[...] = jnp.zeros_like(st_ref)

    c = CHUNK
    t_blk, dk = qs_ref.shape
    n_lvl = m_ref.shape[0]
    lb = lb_ref[...]
    z = f_ref[...].astype(F32)
    qs_ref[...] = _silu(q_ref[...].astype(F32))
    ks_ref[...] = (1.0 - lb) * _sigmoid(-z)
    hi, lo = _split2(jnp.log(lb + (1.0 - lb) * _sigmoid(z)) * LOG2_E)
    lf_ref[...] = jnp.concatenate([hi, lo], axis=-1)

    chunks = [slice(ci * c, (ci + 1) * c) for ci in range(t_blk // c)]
    n_rows = w_ref.shape[0]
    for ci, rows in enumerate(chunks):
        g2 = _dot(w_ref[...], lf_ref[rows, :])
        e_ref[ci * n_rows:(ci + 1) * n_rows, :] = jnp.exp2(g2[:, :dk] + g2[:, dk:])
    for ci, rows in enumerate(chunks):
        qc, kc = qs_ref[rows, :], ks_ref[rows, :]
        s = m_ref[0] * _dot_nt(qc.astype(BF16), kc.astype(BF16))
        for l in range(1, n_lvl):
            el = e_ref[ci * n_rows + (l + 1) * c:ci * n_rows + (l + 2) * c, :]
            s = s + m_ref[l] * _dot_nt((qc * el).astype(BF16), (kc * el).astype(BF16))
        s_ref[rows, :] = s.astype(BF16)
    for ci, rows in enumerate(chunks):
        vc = i_ref[rows, :]
        ob_ref[rows, :] = _dot(s_ref[rows, :], vc)
        e_end = e_ref[ci * n_rows + c:ci * n_rows + 2 * c, :]
        kv_ref[ci] = _dot_tn(vc, (ks_ref[rows, :] * e_end).astype(BF16))
    st = st_ref[...]
    for ci, rows in enumerate(chunks):
        e_start = e_ref[ci * n_rows:ci * n_rows + c, :]
        ob_ref[rows, :] += _dot_nt((qs_ref[rows, :] * e_start).astype(BF16), st.astype(BF16))
        st = st * e_start[c - 1:c] + kv_ref[ci]
    st_ref[...] = st

    o = ob_ref[...]
    o = o * lax.rsqrt(jnp.mean(o * o, axis=-1, keepdims=True) + EPS)
    o = o * nw_ref[...] * _silu(g_ref[...].astype(F32))
    o_ref[...] = (_sigmoid(gm_ref[...].astype(F32)) * o).astype(o_ref.dtype)


def _hgrn2(z, lb, hg_norm_w, batch, seq, t_blk=512):
    n = batch * seq
    d = hg_norm_w.shape[0]
    dk = d // HG_HEADS
    nt = seq // t_blk
    wmat, mask = _hgrn_level_tables()

    def part(p):
        return pl.BlockSpec((None, t_blk, LANES), lambda b, h, t, p=p: (p * HG_HEADS + h, b * nt + t, 0))

    head_vec = pl.BlockSpec((1, dk), lambda b, h, t: (0, h))
    return pl.pallas_call(
        _hgrn_kernel,
        out_shape=jax.ShapeDtypeStruct((n, d), BF16),
        grid=(batch, HG_HEADS, nt),
        in_specs=[
            part(P_HQ), part(P_HF), part(P_HI), part(P_HG), part(P_GH), head_vec, head_vec,
            pl.BlockSpec(wmat.shape, lambda b, h, t: (0, 0)),
            pl.BlockSpec(mask.shape, lambda b, h, t: (0, 0, 0)),
        ],
        out_specs=pl.BlockSpec((t_blk, dk), lambda b, h, t: (b * nt + t, h)),
        scratch_shapes=[
            pltpu.VMEM((dk, dk), F32), pltpu.VMEM((t_blk, dk), F32), pltpu.VMEM((t_blk, dk), F32),
            pltpu.VMEM((t_blk, 2 * dk), BF16), pltpu.VMEM((t_blk, dk), F32),
            pltpu.VMEM((t_blk // CHUNK * wmat.shape[0], dk), F32), pltpu.VMEM((t_blk, CHUNK), BF16),
            pltpu.VMEM((t_blk // CHUNK, dk, dk), F32),
        ],
        compiler_params=pltpu.CompilerParams(
            dimension_semantics=("parallel", "parallel", "arbitrary"), vmem_limit_bytes=VMEM_LIMIT),
        name="hgrn2",
    )(z, z, z, z, z, lb.reshape(1, d), hg_norm_w.reshape(1, d), wmat, mask)


def _outproj_kernel(a_ref, b_ref, x_ref, w_ref, nw_ref, wr_ref, br_ref, h_ref, hn_ref, gate_ref, idx_ref):
    merged = (a_ref[...].astype(F32) + b_ref[...].astype(F32)).astype(BF16)
    h = x_ref[...] + _dot(merged, w_ref[...])
    h_ref[...] = h
    hn = h * lax.rsqrt(jnp.mean(h * h, axis=-1, keepdims=True) + EPS) * nw_ref[...]
    hn_ref[...] = hn
    logits = jnp.dot(hn, wr_ref[...], preferred_element_type=F32,
                     precision=lax.Precision.HIGHEST) + br_ref[...]
    n_e = logits.shape[-1]
    lane = lax.broadcasted_iota(jnp.int32, logits.shape, 1).astype(F32)
    vals, idxs = [], []
    for _ in range(TOP_K):
        m = jnp.max(logits, axis=-1, keepdims=True)
        idx = jnp.min(jnp.where(logits == m, lane, float(n_e)), axis=-1, keepdims=True)
        vals.append(m)
        idxs.append(idx)
        logits = jnp.where(lane == idx, -jnp.inf, logits)
    exps = [jnp.exp(v - vals[0]) for v in vals]
    denom = exps[0] + exps[1] + exps[2] + exps[3]
    for k in range(TOP_K):
        gate_ref[:, k:k + 1] = exps[k] / denom
        idx_ref[:, k:k + 1] = idxs[k].astype(jnp.int32)


def _outproj(o_r, o_h, x2, w_bf, norm_w, w_router, b_router, tm=512):
    n, d = x2.shape
    n_e = w_router.shape[1]
    row = lambda w: pl.BlockSpec((tm, w), lambda i: (i, 0))
    full = lambda a: pl.BlockSpec(a.shape, lambda i: (0, 0))
    nw, br = norm_w.reshape(1, d), b_router.reshape(1, n_e)
    return pl.pallas_call(
        _outproj_kernel,
        out_shape=(jax.ShapeDtypeStruct((n, d), F32), jax.ShapeDtypeStruct((n, d), F32),
                   jax.ShapeDtypeStruct((n, TOP_K), F32), jax.ShapeDtypeStruct((n, TOP_K), jnp.int32)),
        grid=(n // tm,),
        in_specs=[row(d), row(d), row(d), full(w_bf), full(nw), full(w_router), full(br)],
        out_specs=(row(d), row(d), row(TOP_K), row(TOP_K)),
        compiler_params=pltpu.CompilerParams(
            dimension_semantics=("parallel",), vmem_limit_bytes=VMEM_LIMIT),
        name="outproj_router",
    )(o_r, o_h, x2, w_bf, nw, w_router, br)


def _rank_kernel(idx_ref, tri_ref, rank_ref, cnt_ref, run_ref):
    @pl.when(pl.program_id(0) == 0)
    def _():
        run_ref[...] = jnp.zeros_like(run_ref)

    idx = idx_ref[...]
    lane = lax.broadcasted_iota(jnp.int32, (idx.shape[0], N_EXPERTS), 1)
    run = run_ref[...]
    for k in range(TOP_K):
        onehot = lane == idx[:, k:k + 1]
        before = _dot(tri_ref[...], onehot.astype(BF16)) + run
        rank_ref[:, k:k + 1] = jnp.sum(jnp.where(onehot, before, 0.0), axis=-1, keepdims=True).astype(jnp.int32)
        run = run + jnp.sum(onehot.astype(F32), axis=0, keepdims=True)
    run_ref[...] = run
    cnt_ref[...] = run.astype(jnp.int32)


def _pair_ranks(idx, blk=512):
    n = idx.shape[0]
    r = np.arange(blk)
    tri = jnp.asarray(r[None, :] < r[:, None], BF16)
    return pl.pallas_call(
        _rank_kernel,
        out_shape=(jax.ShapeDtypeStruct((n, TOP_K), jnp.int32), jax.ShapeDtypeStruct((1, N_EXPERTS), jnp.int32)),
        grid=(n // blk,),
        in_specs=[pl.BlockSpec((blk, TOP_K), lambda i: (i, 0)), pl.BlockSpec((blk, blk), lambda i: (0, 0))],
        out_specs=(pl.BlockSpec((blk, TOP_K), lambda i: (i, 0)), pl.BlockSpec((1, N_EXPERTS), lambda i: (0, 0))),
        scratch_shapes=[pltpu.VMEM((1, N_EXPERTS), F32)],
        compiler_params=pltpu.CompilerParams(dimension_semantics=("arbitrary",)),
        name="pair_ranks",
    )(idx, tri)


def _dispatch_kernel(dest_ref, src_ref, dst_ref, sem, *, tc):
    base = pl.program_id(0) * tc * TOP_K

    def issue(t, carry):
        for k in range(TOP_K):
            pltpu.make_async_copy(src_ref.at[pl.ds(t, 1)],
                                  dst_ref.at[pl.ds(dest_ref[base + t * TOP_K + k], 1)], sem).start()
        return carry

    lax.fori_loop(0, tc, issue, 0)
    for _ in range(TOP_K):
        pltpu.make_async_copy(src_ref, dst_ref.at[pl.ds(0, tc)], sem).wait()


def _dispatch_rows(src, dest, m_pad, tc=256):
    n, d = src.shape
    return pl.pallas_call(
        functools.partial(_dispatch_kernel, tc=tc),
        out_shape=jax.ShapeDtypeStruct((m_pad, d), src.dtype),
        grid_spec=pltpu.PrefetchScalarGridSpec(
            num_scalar_prefetch=1, grid=(n // tc,),
            in_specs=[pl.BlockSpec((tc, d), lambda i, dref: (i, 0))],
            out_specs=pl.BlockSpec(memory_space=pl.ANY),
            scratch_shapes=[pltpu.SemaphoreType.DMA]),
        compiler_params=pltpu.CompilerParams(
            dimension_semantics=("arbitrary",), vmem_limit_bytes=VMEM_LIMIT),
        name="dispatch_rows",
    )(dest, src)


PAIR = 2 * LANES


def _pair_permutation():
    c = np.arange(PAIR)
    p = np.zeros((PAIR, PAIR), np.float32)
    p[c, c // 2 + LANES * (c % 2)] = 1.0
    return jnp.asarray(p, BF16)


def _w1_prep_kernel(w_ref, p_ref, o_ref):
    for g in range(w_ref.shape[1] // PAIR):
        cols = slice(g * PAIR, (g + 1) * PAIR)
        o_ref[:, cols] = _dot(w_ref[:, cols].astype(BF16), p_ref[...]).astype(BF16)


def _w1_prep(w1, rt=1024, ct=2048):
    e, d, c2 = w1.shape
    ct = min(ct, c2)
    return pl.pallas_call(
        _w1_prep_kernel,
        out_shape=jax.ShapeDtypeStruct(w1.shape, BF16),
        grid=(e, d // rt, c2 // ct),
        in_specs=[pl.BlockSpec((None, rt, ct), lambda a, b, c: (a, b, c)),
                  pl.BlockSpec((PAIR, PAIR), lambda a, b, c: (0, 0))],
        out_specs=pl.BlockSpec((None, rt, ct), lambda a, b, c: (a, b, c)),
        compiler_params=pltpu.CompilerParams(
            dimension_semantics=("parallel", "parallel", "parallel"), vmem_limit_bytes=VMEM_LIMIT),
        name="w1_prep",
    )(w1, _pair_permutation())


def _expert_kernel(be_ref, nv_ref, nb_ref, x_ref, w1_ref, b1_ref, w2_ref, b2_ref, y_ref, xb_ref):
    i, j = pl.program_id(0), pl.program_id(1)

    @pl.when(i < nb_ref[0])
    def _():
        @pl.when(j == 0)
        def _():
            row = lax.broadcasted_iota(jnp.int32, (x_ref.shape[0], 1), 0)
            xb_ref[...] = jnp.where(row < nv_ref[i], x_ref[...], 0.0).astype(BF16)

        hcat = _dot(xb_ref[...], w1_ref[...]) + b1_ref[...]
        hid = []
        for g in range(hcat.shape[1] // PAIR):
            glu = jnp.minimum(hcat[:, g * PAIR:g * PAIR + LANES], SWIGLU_LIMIT)
            lin = jnp.clip(hcat[:, g * PAIR + LANES:(g + 1) * PAIR], -SWIGLU_LIMIT, SWIGLU_LIMIT)
            hid.append((glu * _sigmoid(SWIGLU_ALPHA * glu) * (lin + 1.0)).astype(BF16))
        part = _dot(jnp.concatenate(hid, axis=-1), w2_ref[...])

        @pl.when(j == 0)
        def _():
            y_ref[...] = part + b2_ref[...]

        @pl.when(j > 0)
        def _():
            y_ref[...] += part


def _experts(xs, block_expert, block_valid, n_blocks_used, w1p, b1p, w2b, b2, tm, tf=512):
    m_pad, d = xs.shape
    f = w2b.shape[1]
    nf = f // tf
    nblk = m_pad // tm

    def blk(i, nb):
        return jnp.minimum(i, nb[0] - 1)

    def ftile(i, j, nb):
        return jnp.where(i < nb[0], j, nf - 1)

    return pl.pallas_call(
        _expert_kernel,
        out_shape=jax.ShapeDtypeStruct((m_pad, d), F32),
        grid_spec=pltpu.PrefetchScalarGridSpec(
            num_scalar_prefetch=3, grid=(nblk, nf),
            in_specs=[
                pl.BlockSpec((tm, d), lambda i, j, be, nv, nb: (blk(i, nb), 0)),
                pl.BlockSpec((None, d, 2 * tf), lambda i, j, be, nv, nb: (be[blk(i, nb)], 0, ftile(i, j, nb))),
                pl.BlockSpec((None, 1, 2 * tf), lambda i, j, be, nv, nb: (be[blk(i, nb)], 0, ftile(i, j, nb))),
                pl.BlockSpec((None, tf, d), lambda i, j, be, nv, nb: (be[blk(i, nb)], ftile(i, j, nb), 0)),
                pl.BlockSpec((None, 1, d), lambda i, j, be, nv, nb: (be[blk(i, nb)], 0, 0)),
            ],
            out_specs=pl.BlockSpec((tm, d), lambda i, j, be, nv, nb: (blk(i, nb), 0)),
            scratch_shapes=[pltpu.VMEM((tm, d), BF16)]),
        compiler_params=pltpu.CompilerParams(
            dimension_semantics=("arbitrary", "arbitrary"), vmem_limit_bytes=VMEM_LIMIT),
        name="experts",
    )(block_expert, block_valid, n_blocks_used, xs, w1p, b1p, w2b, b2)


def _combine_kernel(dest_ref, h_ref, gate_ref, nw_ref, y_ref, o_ref, buf_ref, sem, *, tc):
    base = pl.program_id(0) * tc * TOP_K

    def issue(t, carry):
        for k in range(TOP_K):
            pltpu.make_async_copy(y_ref.at[pl.ds(dest_ref[base + t * TOP_K + k], 1)],
                                  buf_ref.at[k, pl.ds(t, 1)], sem).start()
        return carry

    lax.fori_loop(0, tc, issue, 0)
    for k in range(TOP_K):
        pltpu.make_async_copy(y_ref.at[pl.ds(0, tc)], buf_ref.at[k], sem).wait()
    gate = gate_ref[...]
    out = h_ref[...]
    for k in range(TOP_K):
        out = out + gate[:, k:k + 1] * buf_ref[k]
    out = out * lax.rsqrt(jnp.mean(out * out, axis=-1, keepdims=True) + EPS) * nw_ref[...]
    o_ref[...] = out


def _combine(dest, h, gates, norm_w, y, tc=128):
    n, d = h.shape
    return pl.pallas_call(
        functools.partial(_combine_kernel, tc=tc),
        out_shape=jax.ShapeDtypeStruct((n, d), F32),
        grid_spec=pltpu.PrefetchScalarGridSpec(
            num_scalar_prefetch=1, grid=(n // tc,),
            in_specs=[
                pl.BlockSpec((tc, d), lambda i, dref: (i, 0)),
                pl.BlockSpec((tc, TOP_K), lambda i, dref: (i, 0)),
                pl.BlockSpec((1, d), lambda i, dref: (0, 0)),
                pl.BlockSpec(memory_space=pl.ANY),
            ],
            out_specs=pl.BlockSpec((tc, d), lambda i, dref: (i, 0)),
            scratch_shapes=[pltpu.VMEM((TOP_K, tc, d), F32), pltpu.SemaphoreType.DMA]),
        compiler_params=pltpu.CompilerParams(
            dimension_semantics=("arbitrary",), vmem_limit_bytes=VMEM_LIMIT),
        name="combine_norm",
    )(dest, h, gates, norm_w.reshape(1, d), y)


def _moe(h, hn, gates, idx, w1, b1, w2, b2, norm_final_w, tm=512, tf=512):
    n, d = h.shape
    n_pairs = n * TOP_K
    e_flat = idx.reshape(n_pairs)
    rank, counts = _pair_ranks(idx, blk=min(512, n))
    counts = counts.reshape(N_EXPERTS)
    padded = (counts + tm - 1) // tm * tm
    end_pad = jnp.cumsum(padded)
    start_pad = end_pad - padded
    dest = start_pad[e_flat] + rank.reshape(n_pairs)
    dest = dest.astype(jnp.int32)
    m_pad = n_pairs + N_EXPERTS * tm
    n_blocks = m_pad // tm
    block_start = jnp.arange(n_blocks, dtype=jnp.int32) * tm
    block_expert = jnp.minimum(jnp.sum(end_pad[None, :] <= block_start[:, None], axis=1),
                               N_EXPERTS - 1).astype(jnp.int32)
    block_valid = jnp.clip((start_pad + counts)[block_expert] - block_start, 0, tm).astype(jnp.int32)
    n_blocks_used = (end_pad[-1:] // tm).astype(jnp.int32)
    xs = _dispatch_rows(hn, dest, m_pad, tc=min(256, n))
    f = w2.shape[1]
    b1p = b1.reshape(N_EXPERTS, f // LANES, LANES, 2).transpose(0, 1, 3, 2).reshape(N_EXPERTS, 1, 2 * f)
    y = _experts(xs, block_expert, block_valid, n_blocks_used, _w1_prep(w1), b1p, w2.astype(BF16),
                 b2[:, None, :], tm, tf)
    return _combine(dest, h, gates, norm_final_w, y)


def kernel(x, w_in, ret_norm_w, hg_norm_w, hg_lb_logits, w_out, norm_mix_w, norm_moe_w, w_router, b_router,
           w1, b1, w2, b2, norm_final_w):
    batch, seq, d = x.shape
    depth = w_in.shape[0]
    assert depth == 1, "the final norm is fused into the last layer's combine; one layer supported"
    lb_all = jnp.cumsum(jax.nn.softmax(hg_lb_logits.astype(F32), axis=0), axis=0)
    x2 = x.reshape(batch * seq, d)
    z = _inproj(x2, norm_mix_w[0], w_in[0].astype(BF16))
    o_r = _retention(z, ret_norm_w[0], batch, seq)
    o_h = _hgrn2(z, lb_all[0], hg_norm_w[0], batch, seq)
    h, hn, gates, idx = _outproj(o_r, o_h, x2, w_out[0].astype(BF16), norm_moe_w[0], w_router[0], b_router[0])
    out = _moe(h, hn, gates, idx, w1[0], b1[0], w2[0], b2[0], norm_final_w)
    return out.reshape(batch, seq, d)
```

```python
import functools

import numpy as np
import jax
import jax.numpy as jnp
from jax import lax
from jax.experimental import pallas as pl
from jax.experimental.pallas import tpu as pltpu

F32 = jnp.float32
BF16 = jnp.bfloat16

LANES = 128
SUBLANES = 8
CHUNK = 64
HG_LEVELS = (32, 16, 8, 4, 2, 1)
RET_HEADS = 8
HG_HEADS = 16
N_SPLITS = 10
N_EXPERTS = 32
TOP_K = 4
SWIGLU_ALPHA = 1.702
SWIGLU_LIMIT = 7.0
ROPE_BASE = 10000.0
EPS = 1e-6
LOG2_E = 1.4426950408889634
VMEM_LIMIT = 56 * 1024 * 1024

P_RQ, P_RK, P_RV, P_RG, P_HQ, P_HF, P_HI, P_HG, P_GR, P_GH = range(N_SPLITS)


def _sigmoid(x):
    return 1.0 / (1.0 + jnp.exp(-x))


def _sigmoid_pair(x):
    t = jnp.exp(-jnp.abs(x))
    r = 1.0 / (1.0 + t)
    p = t * r
    pos = x >= 0.0
    return jnp.where(pos, r, p), jnp.where(pos, p, r)


def _silu(x):
    return x * _sigmoid(x)


def _dot(a, b):
    return jnp.dot(a, b, preferred_element_type=F32)


def _dot_nt(a, b):
    return lax.dot_general(a, b, (((1,), (1,)), ((), ())), preferred_element_type=F32)


def _dot_tn(a, b):
    return lax.dot_general(a, b, (((0,), (0,)), ((), ())), preferred_element_type=F32)


def _inproj_kernel(x_ref, nw_ref, w_ref, o_ref, xn_ref):
    @pl.when(pl.program_id(1) == 0)
    def _():
        x = x_ref[...]
        ms = jnp.mean(x * x, axis=-1, keepdims=True)
        xn_ref[...] = (x * lax.rsqrt(ms + EPS) * nw_ref[...]).astype(BF16)

    acc = _dot(xn_ref[...], w_ref[...])
    for g in range(o_ref.shape[0]):
        o_ref[g] = acc[:, g * LANES:(g + 1) * LANES].astype(o_ref.dtype)


def _inproj(x2, norm_w, w_bf, tm=1024, tn=1024):
    n, d = x2.shape
    cols = w_bf.shape[1]
    gpt = tn // LANES
    return pl.pallas_call(
        _inproj_kernel,
        out_shape=jax.ShapeDtypeStruct((cols // LANES, n, LANES), BF16),
        grid=(n // tm, cols // tn),
        in_specs=[
            pl.BlockSpec((tm, d), lambda i, j: (i, 0)),
            pl.BlockSpec((1, d), lambda i, j: (0, 0)),
            pl.BlockSpec((d, tn), lambda i, j: (0, j)),
        ],
        out_specs=pl.BlockSpec((gpt, tm, LANES), lambda i, j: (j, i, 0)),
        scratch_shapes=[pltpu.VMEM((tm, d), BF16)],
        compiler_params=pltpu.CompilerParams(
            dimension_semantics=("parallel", "arbitrary"), vmem_limit_bytes=VMEM_LIMIT),
        name="inproj",
    )(x2, norm_w.reshape(1, d), w_bf)


def _retention_tables(seq, t_blk, dk):
    h = RET_HEADS
    log_gamma = jnp.log1p(-jnp.exp2(-5.0 - jnp.arange(h, dtype=F32)))
    inv = 1.0 / (ROPE_BASE ** jnp.linspace(0.0, 1.0, dk // 2, dtype=F32))
    ang = jnp.arange(seq, dtype=F32)[:, None] * inv[None, :]
    cos, sin = jnp.cos(ang), jnp.sin(ang)
    pos = np.arange(t_blk)
    cq, ck = pos[:, None] // CHUNK, pos[None, :] // CHUNK
    dist = (pos[:, None] - pos[None, :]).astype(np.float32)
    expo = np.where(cq == ck, np.abs(dist), dist)
    allowed = jnp.asarray(cq >= ck)
    dmask = jnp.where(allowed[None], jnp.exp(jnp.asarray(expo)[None] * log_gamma[:, None, None]), 0.0)
    posf = jnp.arange(t_blk, dtype=F32)
    qdec = jnp.exp((posf[None, :] + 1.0) * log_gamma[:, None])[..., None]
    kdec = jnp.exp((t_blk - 1.0 - posf)[None, :] * log_gamma[:, None])[..., None]
    sdec = jnp.broadcast_to(jnp.exp(t_blk * log_gamma)[:, None, None], (h, 1, LANES))
    return cos, sin, dmask, qdec, kdec, sdec


def _retention_kernel(q_ref, k_ref, v_ref, g_ref, gm_ref, cos_ref, sin_ref, dm_ref, qd_ref, kd_ref,
                      sd_ref, nw_ref, o_ref, st_ref, *, scale):
    @pl.when(pl.program_id(2) == 0)
    def _():
        st_ref[...] = jnp.zeros_like(st_ref)

    cos, sin = cos_ref[...], sin_ref[...]

    def rot(ref):
        x1, x2 = ref[0].astype(F32), ref[1].astype(F32)
        return jnp.concatenate([x1 * cos - x2 * sin, x2 * cos + x1 * sin], axis=-1)

    q = rot(q_ref)
    k = rot(k_ref) * scale
    v = jnp.concatenate([v_ref[0], v_ref[1]], axis=-1)
    q_bf, k_bf = q.astype(BF16), k.astype(BF16)
    scores = _dot_nt(q_bf, k_bf) * dm_ref[...]
    state = st_ref[...]
    o = _dot(scores.astype(BF16), v) + _dot((q * qd_ref[...]).astype(BF16), state.astype(BF16))
    st_ref[...] = sd_ref[:, 0:1] * state + _dot_tn((k * kd_ref[...]).astype(BF16), v)

    o = o - jnp.mean(o, axis=-1, keepdims=True)
    o = o * lax.rsqrt(jnp.mean(o * o, axis=-1, keepdims=True) + EPS)
    gate = jnp.concatenate([g_ref[0], g_ref[1]], axis=-1).astype(F32)
    merge = jnp.concatenate([gm_ref[0], gm_ref[1]], axis=-1).astype(F32)
    o_ref[...] = (_sigmoid(merge) * (o * nw_ref[...] * _silu(gate))).astype(o_ref.dtype)


def _retention(z, ret_norm_w, batch, seq, t_blk=512):
    n = batch * seq
    d = ret_norm_w.shape[0]
    dk = d // RET_HEADS
    gph = dk // LANES
    nt = seq // t_blk
    cos, sin, dmask, qdec, kdec, sdec = _retention_tables(seq, t_blk, dk)

    def part(p):
        return pl.BlockSpec((gph, t_blk, LANES), lambda b, h, t, p=p: (p * RET_HEADS + h, b * nt + t, 0))

    per_head = lambda shape: pl.BlockSpec((None,) + shape, lambda b, h, t: (h, 0, 0))
    return pl.pallas_call(
        functools.partial(_retention_kernel, scale=dk ** -0.5),
        out_shape=jax.ShapeDtypeStruct((n, d), BF16),
        grid=(batch, RET_HEADS, nt),
        in_specs=[
            part(P_RQ), part(P_RK), part(P_RV), part(P_RG), part(P_GR),
            pl.BlockSpec((t_blk, dk // 2), lambda b, h, t: (t, 0)),
            pl.BlockSpec((t_blk, dk // 2), lambda b, h, t: (t, 0)),
            per_head((t_blk, t_blk)), per_head((t_blk, 1)), per_head((t_blk, 1)), per_head((1, LANES)),
            pl.BlockSpec((1, dk), lambda b, h, t: (0, h)),
        ],
        out_specs=pl.BlockSpec((t_blk, dk), lambda b, h, t: (b * nt + t, h)),
        scratch_shapes=[pltpu.VMEM((dk, dk), F32)],
        compiler_params=pltpu.CompilerParams(
            dimension_semantics=("parallel", "parallel", "arbitrary"), vmem_limit_bytes=VMEM_LIMIT),
        name="retention",
    )(z, z, z, z, z, cos, sin, dmask, qdec, kdec, sdec, ret_norm_w.reshape(1, d))


def _hgrn_level_tables():
    c = CHUNK
    r = np.arange(c)
    mats = [(r[None, :] <= r[:, None])]
    masks = [np.eye(c, dtype=bool)]
    for s in HG_LEVELS:
        blk, within = r // (2 * s), r % (2 * s)
        mid = blk * 2 * s + s - 1
        right = within >= s
        t = r[None, :]
        if s < SUBLANES:
            mats.append(np.where(right[:, None], (t > mid[:, None]) & (t <= r[:, None]),
                                 (t > r[:, None]) & (t <= mid[:, None])))
        masks.append((blk[:, None] == blk[None, :]) & right[:, None] & (~right)[None, :])
    wmat = np.concatenate(mats, axis=0).astype(np.float32)
    if len(masks) % 2:
        masks.append(np.zeros((c, c), bool))
    mask = np.stack([np.concatenate(masks[p:p + 2], axis=1) for p in range(0, len(masks), 2)])
    return jnp.asarray(wmat, BF16), jnp.asarray(mask.astype(np.float32), F32)


def _split2(x):
    hi = x.astype(BF16)
    return hi, (x - hi.astype(F32)).astype(BF16)


def _level_exponent(cum, s):
    pieces = []
    for r0 in range(0, CHUNK, SUBLANES):
        mid = r0 // (2 * s) * 2 * s + s - 1
        rows, ref = cum[r0:r0 + SUBLANES], cum[mid:mid + 1]
        pieces.append(rows - ref if r0 % (2 * s) >= s else ref - rows)
    return jnp.concatenate(pieces, axis=0)


def _hgrn_kernel(q_ref, f_ref, i_ref, g_ref, gm_ref, lb_ref, nw_ref, w_ref, m_ref, o_ref,
                 st_ref, qs_ref, ks_ref, ob_ref, qe_ref, kd_ref, dec_ref, el_ref, s_ref, kv_ref):
    @pl.when(pl.program_id(2) == 0)
    def _():
        st_ref[...] = jnp.zeros_like(st_ref)

    c = CHUNK
    t_blk, dk = qs_ref.shape
    n_lvl = len(HG_LEVELS)
    n_chunks = t_blk // c
    lb = lb_ref[...]
    zeros = jnp.zeros((c, dk), BF16)
    state = [st_ref[...]]

    def stage_gates(ci):
        rows = slice(ci * c, (ci + 1) * c)
        z = f_ref[rows, :].astype(F32)
        qs = _silu(q_ref[rows, :].astype(F32))
        sig_pos, sig_neg = _sigmoid_pair(z)
        ks = (1.0 - lb) * sig_neg
        qs_ref[rows, :] = qs.astype(BF16)
        ks_ref[rows, :] = ks.astype(BF16)
        hi, lo = _split2(jnp.log(lb + (1.0 - lb) * sig_pos) * LOG2_E)
        g2 = _dot(w_ref[...], jnp.concatenate([hi, lo], axis=-1))
        g = g2[:, :dk] + g2[:, dk:]
        cum = g[0:c]
        e_start = jnp.exp2(cum)
        qe_ref[rows, :] = (qs * e_start).astype(BF16)
        kd_ref[rows, :] = (ks * jnp.exp2(cum[c - 1:c] - cum)).astype(BF16)
        dec_ref[ci] = e_start[c - 1:c]
        narrow = 1
        for l, s in enumerate(HG_LEVELS):
            if s >= SUBLANES:
                expo = _level_exponent(cum, s)
            else:
                expo = g[narrow * c:(narrow + 1) * c]
                narrow += 1
            el_ref[(ci * n_lvl + l) * c:(ci * n_lvl + l + 1) * c, :] = jnp.exp2(expo).astype(BF16)

    def stage_scores(ci):
        rows = slice(ci * c, (ci + 1) * c)
        qb, kb = qs_ref[rows, :], ks_ref[rows, :]
        lhs, rhs = [qb], [kb]
        for l in range(n_lvl):
            el = el_ref[(ci * n_lvl + l) * c:(ci * n_lvl + l + 1) * c, :]
            lhs.append(qb * el)
            rhs.append(kb * el)
        s = None
        for p in range(m_ref.shape[0]):
            a0, b0 = lhs[2 * p], rhs[2 * p]
            if 2 * p + 1 < len(lhs):
                a = jnp.concatenate([a0, lhs[2 * p + 1]], axis=1)
                b = jnp.concatenate([jnp.concatenate([b0, zeros], axis=1),
                                     jnp.concatenate([zeros, rhs[2 * p + 1]], axis=1)], axis=0)
            else:
                a, b = a0, jnp.concatenate([b0, zeros], axis=0)
            tile = m_ref[p] * _dot_nt(a, b)
            s = tile if s is None else s + tile
        s_ref[rows, :] = s.astype(BF16)

    def stage_intra(ci):
        rows = slice(ci * c, (ci + 1) * c)
        vc = i_ref[rows, :]
        ob_ref[rows, :] = _dot(s_ref[rows, :], jnp.concatenate([vc, vc], axis=0))
        kv_ref[ci] = _dot_tn(vc, kd_ref[rows, :])

    def stage_state(ci):
        rows = slice(ci * c, (ci + 1) * c)
        st = state[0]
        o = ob_ref[rows, :] + _dot_nt(qe_ref[rows, :], st.astype(BF16))
        state[0] = st * dec_ref[ci] + kv_ref[ci]
        o = o * lax.rsqrt(jnp.mean(o * o, axis=-1, keepdims=True) + EPS)
        o = o * nw_ref[...] * _silu(g_ref[rows, :].astype(F32))
        o_ref[rows, :] = (_sigmoid(gm_ref[rows, :].astype(F32)) * o).astype(o_ref.dtype)

    stages = (stage_gates, stage_scores, stage_intra, stage_state)
    for step in range(n_chunks + len(stages) - 1):
        for depth, stage in enumerate(stages):
            if 0 <= step - depth < n_chunks:
                stage(step - depth)
    st_ref[...] = state[0]


def _hgrn2(z, lb, hg_norm_w, batch, seq, t_blk=1024):
    n = batch * seq
    d = hg_norm_w.shape[0]
    dk = d // HG_HEADS
    nt = seq // t_blk
    wmat, mask = _hgrn_level_tables()

    def part(p):
        return pl.BlockSpec((None, t_blk, LANES), lambda b, h, t, p=p: (p * HG_HEADS + h, b * nt + t, 0))

    head_vec = pl.BlockSpec((1, dk), lambda b, h, t: (0, h))
    return pl.pallas_call(
        _hgrn_kernel,
        out_shape=jax.ShapeDtypeStruct((n, d), BF16),
        grid=(batch, HG_HEADS, nt),
        in_specs=[
            part(P_HQ), part(P_HF), part(P_HI), part(P_HG), part(P_GH), head_vec, head_vec,
            pl.BlockSpec(wmat.shape, lambda b, h, t: (0, 0)),
            pl.BlockSpec(mask.shape, lambda b, h, t: (0, 0, 0)),
        ],
        out_specs=pl.BlockSpec((t_blk, dk), lambda b, h, t: (b * nt + t, h)),
        scratch_shapes=[
            pltpu.VMEM((dk, dk), F32), pltpu.VMEM((t_blk, dk), BF16), pltpu.VMEM((t_blk, dk), BF16),
            pltpu.VMEM((t_blk, dk), F32),
            pltpu.VMEM((t_blk, dk), BF16), pltpu.VMEM((t_blk, dk), BF16),
            pltpu.VMEM((t_blk // CHUNK, 1, dk), F32),
            pltpu.VMEM((t_blk * len(HG_LEVELS), dk), BF16), pltpu.VMEM((t_blk, 2 * CHUNK), BF16),
            pltpu.VMEM((t_blk // CHUNK, dk, dk), F32),
        ],
        compiler_params=pltpu.CompilerParams(
            dimension_semantics=("parallel", "parallel", "arbitrary"), vmem_limit_bytes=VMEM_LIMIT),
        name="hgrn2",
    )(z, z, z, z, z, lb.reshape(1, d), hg_norm_w.reshape(1, d), wmat, mask)


def _outproj_kernel(a_ref, b_ref, x_ref, w_ref, nw_ref, wr_ref, br_ref, h_ref, hn_ref, gate_ref, idx_ref):
    merged = (a_ref[...].astype(F32) + b_ref[...].astype(F32)).astype(BF16)
    h = x_ref[...] + _dot(merged, w_ref[...])
    h_ref[...] = h
    hn = h * lax.rsqrt(jnp.mean(h * h, axis=-1, keepdims=True) + EPS) * nw_ref[...]
    hn_ref[...] = hn
    n_e = br_ref.shape[-1]
    hn_hi, hn_lo = _split2(hn)
    l2 = _dot(hn_hi, wr_ref[...])
    logits = l2[:, :n_e] + l2[:, n_e:] + _dot(hn_lo, wr_ref[:, :n_e]) + br_ref[...]
    lane = lax.broadcasted_iota(jnp.int32, logits.shape, 1).astype(F32)
    vals, idxs = [], []
    for _ in range(TOP_K):
        m = jnp.max(logits, axis=-1, keepdims=True)
        idx = jnp.min(jnp.where(logits == m, lane, float(n_e)), axis=-1, keepdims=True)
        vals.append(m)
        idxs.append(idx)
        logits = jnp.where(lane == idx, -jnp.inf, logits)
    exps = [jnp.exp(v - vals[0]) for v in vals]
    denom = exps[0] + exps[1] + exps[2] + exps[3]
    for k in range(TOP_K):
        gate_ref[:, k:k + 1] = exps[k] / denom
        idx_ref[:, k:k + 1] = idxs[k].astype(jnp.int32)


def _outproj(o_r, o_h, x2, w_bf, norm_w, w_router, b_router, tm=512):
    n, d = x2.shape
    n_e = w_router.shape[1]
    row = lambda w: pl.BlockSpec((tm, w), lambda i: (i, 0))
    full = lambda a: pl.BlockSpec(a.shape, lambda i: (0, 0))
    nw, br = norm_w.reshape(1, d), b_router.reshape(1, n_e)
    wr = jnp.concatenate(_split2(w_router), axis=1)
    return pl.pallas_call(
        _outproj_kernel,
        out_shape=(jax.ShapeDtypeStruct((n, d), F32), jax.ShapeDtypeStruct((n, d), F32),
                   jax.ShapeDtypeStruct((n, TOP_K), F32), jax.ShapeDtypeStruct((n, TOP_K), jnp.int32)),
        grid=(n // tm,),
        in_specs=[row(d), row(d), row(d), full(w_bf), full(nw), full(wr), full(br)],
        out_specs=(row(d), row(d), row(TOP_K), row(TOP_K)),
        compiler_params=pltpu.CompilerParams(
            dimension_semantics=("parallel",), vmem_limit_bytes=VMEM_LIMIT),
        name="outproj_router",
    )(o_r, o_h, x2, w_bf, nw, wr, br)


def _rank_kernel(idx_ref, tri_ref, rank_ref, cnt_ref, run_ref):
    @pl.when(pl.program_id(0) == 0)
    def _():
        run_ref[...] = jnp.zeros_like(run_ref)

    idx = idx_ref[...]
    lane = lax.broadcasted_iota(jnp.int32, (idx.shape[0], N_EXPERTS), 1)
    run = run_ref[...]
    for k in range(TOP_K):
        onehot = lane == idx[:, k:k + 1]
        before = _dot(tri_ref[...], onehot.astype(BF16)) + run
        rank_ref[:, k:k + 1] = jnp.sum(jnp.where(onehot, before, 0.0), axis=-1, keepdims=True).astype(jnp.int32)
        run = run + jnp.sum(onehot.astype(F32), axis=0, keepdims=True)
    run_ref[...] = run
    cnt_ref[...] = run.astype(jnp.int32)


def _pair_ranks(idx, blk=512):
    n = idx.shape[0]
    r = np.arange(blk)
    tri = jnp.asarray(r[None, :] < r[:, None], BF16)
    return pl.pallas_call(
        _rank_kernel,
        out_shape=(jax.ShapeDtypeStruct((n, TOP_K), jnp.int32), jax.ShapeDtypeStruct((1, N_EXPERTS), jnp.int32)),
        grid=(n // blk,),
        in_specs=[pl.BlockSpec((blk, TOP_K), lambda i: (i, 0)), pl.BlockSpec((blk, blk), lambda i: (0, 0))],
        out_specs=(pl.BlockSpec((blk, TOP_K), lambda i: (i, 0)), pl.BlockSpec((1, N_EXPERTS), lambda i: (0, 0))),
        scratch_shapes=[pltpu.VMEM((1, N_EXPERTS), F32)],
        compiler_params=pltpu.CompilerParams(dimension_semantics=("arbitrary",)),
        name="pair_ranks",
    )(idx, tri)


def _dispatch_kernel(dest_ref, src_ref, dst_ref, sem, *, tc):
    base = pl.program_id(0) * tc * TOP_K

    def issue(t, carry):
        for k in range(TOP_K):
            pltpu.make_async_copy(src_ref.at[pl.ds(t, 1)],
                                  dst_ref.at[pl.ds(dest_ref[base + t * TOP_K + k], 1)],
                                  sem).start(priority=k % 2)
        return carry

    lax.fori_loop(0, tc, issue, 0)
    for _ in range(TOP_K):
        pltpu.make_async_copy(src_ref, dst_ref.at[pl.ds(0, tc)], sem).wait()


def _dispatch_rows(src, dest, m_pad, tc=256):
    n, d = src.shape
    return pl.pallas_call(
        functools.partial(_dispatch_kernel, tc=tc),
        out_shape=jax.ShapeDtypeStruct((m_pad, d), src.dtype),
        grid_spec=pltpu.PrefetchScalarGridSpec(
            num_scalar_prefetch=1, grid=(n // tc,),
            in_specs=[pl.BlockSpec((tc, d), lambda i, dref: (i, 0))],
            out_specs=pl.BlockSpec(memory_space=pl.ANY),
            scratch_shapes=[pltpu.SemaphoreType.DMA]),
        compiler_params=pltpu.CompilerParams(
            dimension_semantics=("arbitrary",), vmem_limit_bytes=VMEM_LIMIT),
        name="dispatch_rows",
    )(dest, src)


PAIR = 2 * LANES


def _pair_permutation():
    c = np.arange(PAIR)
    p = np.zeros((PAIR, PAIR), np.float32)
    p[c, c // 2 + LANES * (c % 2)] = 1.0
    return jnp.asarray(p, BF16)


def _w1_prep_kernel(w_ref, p_ref, o_ref):
    for g in range(w_ref.shape[1] // PAIR):
        cols = slice(g * PAIR, (g + 1) * PAIR)
        o_ref[:, cols] = _dot(w_ref[:, cols].astype(BF16), p_ref[...]).astype(BF16)


def _w1_prep(w1, rt=1024, ct=2048):
    e, d, c2 = w1.shape
    ct = min(ct, c2)
    return pl.pallas_call(
        _w1_prep_kernel,
        out_shape=jax.ShapeDtypeStruct(w1.shape, BF16),
        grid=(e, d // rt, c2 // ct),
        in_specs=[pl.BlockSpec((None, rt, ct), lambda a, b, c: (a, b, c)),
                  pl.BlockSpec((PAIR, PAIR), lambda a, b, c: (0, 0))],
        out_specs=pl.BlockSpec((None, rt, ct), lambda a, b, c: (a, b, c)),
        compiler_params=pltpu.CompilerParams(
            dimension_semantics=("parallel", "parallel", "parallel"), vmem_limit_bytes=VMEM_LIMIT),
        name="w1_prep",
    )(w1, _pair_permutation())


def _expert_kernel(be_ref, nv_ref, nb_ref, x_ref, w1_ref, b1_ref, w2_ref, b2_ref, y_ref, xb_ref):
    i, j = pl.program_id(0), pl.program_id(1)

    @pl.when(i < nb_ref[0])
    def _():
        @pl.when(j == 0)
        def _():
            row = lax.broadcasted_iota(jnp.int32, (x_ref.shape[0], 1), 0)
            xb_ref[...] = jnp.where(row < nv_ref[i], x_ref[...], 0.0).astype(BF16)

        hcat = _dot(xb_ref[...], w1_ref[...]) + b1_ref[...]
        hid = []
        for g in range(hcat.shape[1] // PAIR):
            glu = jnp.minimum(hcat[:, g * PAIR:g * PAIR + LANES], SWIGLU_LIMIT)
            lin = jnp.clip(hcat[:, g * PAIR + LANES:(g + 1) * PAIR], -SWIGLU_LIMIT, SWIGLU_LIMIT)
            hid.append((glu * _sigmoid(SWIGLU_ALPHA * glu) * (lin + 1.0)).astype(BF16))
        part = _dot(jnp.concatenate(hid, axis=-1), w2_ref[...].astype(BF16))
        start = jnp.where(j == 0, jnp.broadcast_to(b2_ref[...], y_ref.shape), y_ref[...])
        y_ref[...] = start + part


def _experts(xs, block_expert, block_valid, n_blocks_used, w1p, b1p, w2b, b2, tm, tf=512):
    m_pad, d = xs.shape
    f = w2b.shape[1]
    nf = f // tf
    nblk = m_pad // tm

    def blk(i, nb):
        return jnp.minimum(i, nb[0] - 1)

    def ftile(i, j, nb):
        return jnp.where(i < nb[0], j, nf - 1)

    return pl.pallas_call(
        _expert_kernel,
        out_shape=jax.ShapeDtypeStruct((m_pad, d), F32),
        grid_spec=pltpu.PrefetchScalarGridSpec(
            num_scalar_prefetch=3, grid=(nblk, nf),
            in_specs=[
                pl.BlockSpec((tm, d), lambda i, j, be, nv, nb: (blk(i, nb), 0)),
                pl.BlockSpec((None, d, 2 * tf), lambda i, j, be, nv, nb: (be[blk(i, nb)], 0, ftile(i, j, nb))),
                pl.BlockSpec((None, 1, 2 * tf), lambda i, j, be, nv, nb: (be[blk(i, nb)], 0, ftile(i, j, nb))),
                pl.BlockSpec((None, tf, d), lambda i, j, be, nv, nb: (be[blk(i, nb)], ftile(i, j, nb), 0)),
                pl.BlockSpec((None, 1, d), lambda i, j, be, nv, nb: (be[blk(i, nb)], 0, 0)),
            ],
            out_specs=pl.BlockSpec((tm, d), lambda i, j, be, nv, nb: (blk(i, nb), 0)),
            scratch_shapes=[pltpu.VMEM((tm, d), BF16)]),
        compiler_params=pltpu.CompilerParams(
            dimension_semantics=("arbitrary", "arbitrary"), vmem_limit_bytes=VMEM_LIMIT),
        name="experts",
    )(block_expert, block_valid, n_blocks_used, xs, w1p, b1p, w2b, b2)


def _combine_kernel(dest_ref, h_ref, gate_ref, nw_ref, y_ref, o_ref, buf_ref, sem, *, tc):
    i = pl.program_id(0)
    slot = i % 2

    def fetch(block, into):
        base = block * tc * TOP_K

        def issue(t, carry):
            for k in range(TOP_K):
                pltpu.make_async_copy(y_ref.at[pl.ds(dest_ref[base + t * TOP_K + k], 1)],
                                      buf_ref.at[into, k, pl.ds(t, 1)], sem.at[into]).start(priority=k % 2)
            return carry

        lax.fori_loop(0, tc, issue, 0)

    @pl.when(i == 0)
    def _():
        fetch(0, 0)

    @pl.when(i + 1 < pl.num_programs(0))
    def _():
        fetch(i + 1, 1 - slot)

    for k in range(TOP_K):
        pltpu.make_async_copy(y_ref.at[pl.ds(0, tc)], buf_ref.at[slot, k], sem.at[slot]).wait()
    gate = gate_ref[...]
    out = h_ref[...]
    for k in range(TOP_K):
        out = out + gate[:, k:k + 1] * buf_ref[slot, k]
    out = out * lax.rsqrt(jnp.mean(out * out, axis=-1, keepdims=True) + EPS) * nw_ref[...]
    o_ref[...] = out


def _combine(dest, h, gates, norm_w, y, tc=128):
    n, d = h.shape
    return pl.pallas_call(
        functools.partial(_combine_kernel, tc=tc),
        out_shape=jax.ShapeDtypeStruct((n, d), F32),
        grid_spec=pltpu.PrefetchScalarGridSpec(
            num_scalar_prefetch=1, grid=(n // tc,),
            in_specs=[
                pl.BlockSpec((tc, d), lambda i, dref: (i, 0)),
                pl.BlockSpec((tc, TOP_K), lambda i, dref: (i, 0)),
                pl.BlockSpec((1, d), lambda i, dref: (0, 0)),
                pl.BlockSpec(memory_space=pl.ANY),
            ],
            out_specs=pl.BlockSpec((tc, d), lambda i, dref: (i, 0)),
            scratch_shapes=[pltpu.VMEM((2, TOP_K, tc, d), F32), pltpu.SemaphoreType.DMA((2,))]),
        compiler_params=pltpu.CompilerParams(
            dimension_semantics=("arbitrary",), vmem_limit_bytes=VMEM_LIMIT),
        name="combine_norm",
    )(dest, h, gates, norm_w.reshape(1, d), y)


def _moe(h, hn, gates, idx, w1, b1, w2, b2, norm_final_w, tm=512, tf=512):
    n, d = h.shape
    n_pairs = n * TOP_K
    e_flat = idx.reshape(n_pairs)
    rank, counts = _pair_ranks(idx, blk=min(512, n))
    counts = counts.reshape(N_EXPERTS)
    padded = (counts + tm - 1) // tm * tm
    end_pad = jnp.cumsum(padded)
    start_pad = end_pad - padded
    dest = start_pad[e_flat] + rank.reshape(n_pairs)
    dest = dest.astype(jnp.int32)
    m_pad = n_pairs + N_EXPERTS * tm
    n_blocks = m_pad // tm
    block_start = jnp.arange(n_blocks, dtype=jnp.int32) * tm
    block_expert = jnp.minimum(jnp.sum(end_pad[None, :] <= block_start[:, None], axis=1),
                               N_EXPERTS - 1).astype(jnp.int32)
    block_valid = jnp.clip((start_pad + counts)[block_expert] - block_start, 0, tm).astype(jnp.int32)
    n_blocks_used = (end_pad[-1:] // tm).astype(jnp.int32)
    xs = _dispatch_rows(hn, dest, m_pad, tc=min(256, n))
    f = w2.shape[1]
    b1p = b1.reshape(N_EXPERTS, f // LANES, LANES, 2).transpose(0, 1, 3, 2).reshape(N_EXPERTS, 1, 2 * f)
    y = _experts(xs, block_expert, block_valid, n_blocks_used, _w1_prep(w1), b1p, w2,
                 b2[:, None, :], tm, tf)
    return _combine(dest, h, gates, norm_final_w, y)


def kernel(x, w_in, ret_norm_w, hg_norm_w, hg_lb_logits, w_out, norm_mix_w, norm_moe_w, w_router, b_router,
           w1, b1, w2, b2, norm_final_w):
    batch, seq, d = x.shape
    depth = w_in.shape[0]
    assert depth == 1, "the final norm is fused into the last layer's combine; one layer supported"
    lb_all = jnp.cumsum(jax.nn.softmax(hg_lb_logits.astype(F32), axis=0), axis=0)
    x2 = x.reshape(batch * seq, d)
    z = _inproj(x2, norm_mix_w[0], w_in[0].astype(BF16))
    o_r = _retention(z, ret_norm_w[0], batch, seq)
    o_h = _hgrn2(z, lb_all[0], hg_norm_w[0], batch, seq)
    h, hn, gates, idx = _outproj(o_r, o_h, x2, w_out[0].astype(BF16), norm_moe_w[0], w_router[0], b_router[0])
    out = _moe(h, hn, gates, idx, w1[0], b1[0], w2[0], b2[0], norm_final_w)
    return out.reshape(batch, seq, d)
```

```python
import functools

import numpy as np
import jax
import jax.numpy as jnp
from jax import lax
from jax.experimental import pallas as pl
from jax.experimental.pallas import tpu as pltpu

F32 = jnp.float32
BF16 = jnp.bfloat16

LANES = 128
SUBLANES = 8
CHUNK = 64
HG_LEVELS = (32, 16, 8, 4, 2, 1)
RET_HEADS = 8
HG_HEADS = 16
N_SPLITS = 10
N_EXPERTS = 32
TOP_K = 4
SWIGLU_ALPHA = 1.702
SWIGLU_LIMIT = 7.0
ROPE_BASE = 10000.0
EPS = 1e-6
LOG2_E = 1.4426950408889634
VMEM_LIMIT = 56 * 1024 * 1024

P_RQ, P_RK, P_RV, P_RG, P_HQ, P_HF, P_HI, P_HG, P_GR, P_GH = range(N_SPLITS)


def _sigmoid(x):
    return 1.0 / (1.0 + jnp.exp(-x))


def _sigmoid_pair(x):
    t = jnp.exp(-jnp.abs(x))
    r = 1.0 / (1.0 + t)
    p = t * r
    pos = x >= 0.0
    return jnp.where(pos, r, p), jnp.where(pos, p, r)


def _silu(x):
    return x * _sigmoid(x)


def _dot(a, b):
    return jnp.dot(a, b, preferred_element_type=F32)


def _dot_nt(a, b):
    return lax.dot_general(a, b, (((1,), (1,)), ((), ())), preferred_element_type=F32)


def _dot_tn(a, b):
    return lax.dot_general(a, b, (((0,), (0,)), ((), ())), preferred_element_type=F32)


def _inproj_kernel(x_ref, nw_ref, w_ref, o_ref, xn_ref):
    @pl.when(pl.program_id(1) == 0)
    def _():
        x = x_ref[...]
        ms = jnp.mean(x * x, axis=-1, keepdims=True)
        xn_ref[...] = (x * lax.rsqrt(ms + EPS) * nw_ref[...]).astype(BF16)

    acc = _dot(xn_ref[...], w_ref[...])
    for g in range(o_ref.shape[0]):
        o_ref[g] = acc[:, g * LANES:(g + 1) * LANES].astype(o_ref.dtype)


def _inproj(x2, norm_w, w_bf, tm=1024, tn=1024):
    n, d = x2.shape
    cols = w_bf.shape[1]
    gpt = tn // LANES
    return pl.pallas_call(
        _inproj_kernel,
        out_shape=jax.ShapeDtypeStruct((cols // LANES, n, LANES), BF16),
        grid=(n // tm, cols // tn),
        in_specs=[
            pl.BlockSpec((tm, d), lambda i, j: (i, 0)),
            pl.BlockSpec((1, d), lambda i, j: (0, 0)),
            pl.BlockSpec((d, tn), lambda i, j: (0, j)),
        ],
        out_specs=pl.BlockSpec((gpt, tm, LANES), lambda i, j: (j, i, 0)),
        scratch_shapes=[pltpu.VMEM((tm, d), BF16)],
        compiler_params=pltpu.CompilerParams(
            dimension_semantics=("parallel", "arbitrary"), vmem_limit_bytes=VMEM_LIMIT),
        name="inproj",
    )(x2, norm_w.reshape(1, d), w_bf)


def _retention_tables(seq, t_blk, dk):
    h = RET_HEADS
    log_gamma = jnp.log1p(-jnp.exp2(-5.0 - jnp.arange(h, dtype=F32)))
    inv = 1.0 / (ROPE_BASE ** jnp.linspace(0.0, 1.0, dk // 2, dtype=F32))
    ang = jnp.arange(seq, dtype=F32)[:, None] * inv[None, :]
    cos, sin = jnp.cos(ang), jnp.sin(ang)
    pos = np.arange(t_blk)
    cq, ck = pos[:, None] // CHUNK, pos[None, :] // CHUNK
    dist = (pos[:, None] - pos[None, :]).astype(np.float32)
    expo = np.where(cq == ck, np.abs(dist), dist)
    allowed = jnp.asarray(cq >= ck)
    dmask = jnp.where(allowed[None], jnp.exp(jnp.asarray(expo)[None] * log_gamma[:, None, None]), 0.0)
    posf = jnp.arange(t_blk, dtype=F32)
    qdec = jnp.exp((posf[None, :] + 1.0) * log_gamma[:, None])[..., None]
    kdec = jnp.exp((t_blk - 1.0 - posf)[None, :] * log_gamma[:, None])[..., None]
    sdec = jnp.broadcast_to(jnp.exp(t_blk * log_gamma)[:, None, None], (h, 1, LANES))
    return cos, sin, dmask, qdec, kdec, sdec


def _retention_kernel(q_ref, k_ref, v_ref, g_ref, gm_ref, cos_ref, sin_ref, dm_ref, qd_ref, kd_ref,
                      sd_ref, nw_ref, o_ref, st_ref, *, scale):
    @pl.when(pl.program_id(2) == 0)
    def _():
        st_ref[...] = jnp.zeros_like(st_ref)

    cos, sin = cos_ref[...], sin_ref[...]

    def rot(ref):
        x1, x2 = ref[0].astype(F32), ref[1].astype(F32)
        return jnp.concatenate([x1 * cos - x2 * sin, x2 * cos + x1 * sin], axis=-1)

    q = rot(q_ref)
    k = rot(k_ref) * scale
    v = jnp.concatenate([v_ref[0], v_ref[1]], axis=-1)
    q_bf, k_bf = q.astype(BF16), k.astype(BF16)
    scores = _dot_nt(q_bf, k_bf) * dm_ref[...]
    state = st_ref[...]
    o = _dot(scores.astype(BF16), v) + _dot((q * qd_ref[...]).astype(BF16), state.astype(BF16))
    st_ref[...] = sd_ref[:, 0:1] * state + _dot_tn((k * kd_ref[...]).astype(BF16), v)

    o = o - jnp.mean(o, axis=-1, keepdims=True)
    o = o * lax.rsqrt(jnp.mean(o * o, axis=-1, keepdims=True) + EPS)
    gate = jnp.concatenate([g_ref[0], g_ref[1]], axis=-1).astype(F32)
    merge = jnp.concatenate([gm_ref[0], gm_ref[1]], axis=-1).astype(F32)
    res = (_sigmoid(merge) * (o * nw_ref[...] * _silu(gate))).astype(o_ref.dtype)
    for g in range(o_ref.shape[0]):
        o_ref[g] = res[:, g * LANES:(g + 1) * LANES]


def _retention(z, ret_norm_w, batch, seq, t_blk=512):
    n = batch * seq
    d = ret_norm_w.shape[0]
    dk = d // RET_HEADS
    gph = dk // LANES
    nt = seq // t_blk
    cos, sin, dmask, qdec, kdec, sdec = _retention_tables(seq, t_blk, dk)

    def part(p):
        return pl.BlockSpec((gph, t_blk, LANES), lambda b, h, t, p=p: (p * RET_HEADS + h, b * nt + t, 0))

    per_head = lambda shape: pl.BlockSpec((None,) + shape, lambda b, h, t: (h, 0, 0))
    return pl.pallas_call(
        functools.partial(_retention_kernel, scale=dk ** -0.5),
        out_shape=jax.ShapeDtypeStruct((d // LANES, n, LANES), BF16),
        grid=(batch, RET_HEADS, nt),
        in_specs=[
            part(P_RQ), part(P_RK), part(P_RV), part(P_RG), part(P_GR),
            pl.BlockSpec((t_blk, dk // 2), lambda b, h, t: (t, 0)),
            pl.BlockSpec((t_blk, dk // 2), lambda b, h, t: (t, 0)),
            per_head((t_blk, t_blk)), per_head((t_blk, 1)), per_head((t_blk, 1)), per_head((1, LANES)),
            pl.BlockSpec((1, dk), lambda b, h, t: (0, h)),
        ],
        out_specs=pl.BlockSpec((gph, t_blk, LANES), lambda b, h, t: (h, b * nt + t, 0)),
        scratch_shapes=[pltpu.VMEM((dk, dk), F32)],
        compiler_params=pltpu.CompilerParams(
            dimension_semantics=("parallel", "parallel", "arbitrary"), vmem_limit_bytes=VMEM_LIMIT),
        name="retention",
    )(z, z, z, z, z, cos, sin, dmask, qdec, kdec, sdec, ret_norm_w.reshape(1, d))


def _hgrn_level_tables():
    c = CHUNK
    r = np.arange(c)
    mats = [(r[None, :] <= r[:, None])]
    masks = [np.eye(c, dtype=bool)]
    for s in HG_LEVELS:
        blk, within = r // (2 * s), r % (2 * s)
        mid = blk * 2 * s + s - 1
        right = within >= s
        t = r[None, :]
        if s < SUBLANES:
            mats.append(np.where(right[:, None], (t > mid[:, None]) & (t <= r[:, None]),
                                 (t > r[:, None]) & (t <= mid[:, None])))
        masks.append((blk[:, None] == blk[None, :]) & right[:, None] & (~right)[None, :])
    wmat = np.concatenate(mats, axis=0).astype(np.float32)
    if len(masks) % 2:
        masks.append(np.zeros((c, c), bool))
    mask = np.stack([np.concatenate(masks[p:p + 2], axis=1) for p in range(0, len(masks), 2)])
    return jnp.asarray(wmat, BF16), jnp.asarray(mask.astype(np.float32), F32)


def _split2(x):
    hi = x.astype(BF16)
    return hi, (x - hi.astype(F32)).astype(BF16)


def _level_exponent(cum, s):
    pieces = []
    for r0 in range(0, CHUNK, SUBLANES):
        mid = r0 // (2 * s) * 2 * s + s - 1
        rows, ref = cum[r0:r0 + SUBLANES], cum[mid:mid + 1]
        pieces.append(rows - ref if r0 % (2 * s) >= s else ref - rows)
    return jnp.concatenate(pieces, axis=0)


def _hgrn_kernel(q_ref, f_ref, i_ref, g_ref, gm_ref, lb_ref, nw_ref, w_ref, m_ref, o_ref,
                 st_ref, qs_ref, ks_ref, ob_ref, qe_ref, kd_ref, dec_ref, el_ref, s_ref, kv_ref):
    @pl.when(pl.program_id(2) == 0)
    def _():
        st_ref[...] = jnp.zeros_like(st_ref)

    c = CHUNK
    t_blk, dk = qs_ref.shape
    n_lvl = len(HG_LEVELS)
    n_chunks = t_blk // c
    lb = lb_ref[...]
    zeros = jnp.zeros((c, dk), BF16)
    state = [st_ref[...]]

    def stage_gates(ci):
        rows = slice(ci * c, (ci + 1) * c)
        z = f_ref[rows, :].astype(F32)
        qs = _silu(q_ref[rows, :].astype(F32))
        sig_pos, sig_neg = _sigmoid_pair(z)
        ks = (1.0 - lb) * sig_neg
        qs_ref[rows, :] = qs.astype(BF16)
        ks_ref[rows, :] = ks.astype(BF16)
        hi, lo = _split2(jnp.log(lb + (1.0 - lb) * sig_pos) * LOG2_E)
        g2 = _dot(w_ref[...], jnp.concatenate([hi, lo], axis=-1))
        g = g2[:, :dk] + g2[:, dk:]
        cum = g[0:c]
        e_start = jnp.exp2(cum)
        qe_ref[rows, :] = (qs * e_start).astype(BF16)
        kd_ref[rows, :] = (ks * jnp.exp2(cum[c - 1:c] - cum)).astype(BF16)
        dec_ref[ci] = e_start[c - 1:c]
        narrow = 1
        for l, s in enumerate(HG_LEVELS):
            if s >= SUBLANES:
                expo = _level_exponent(cum, s)
            else:
                expo = g[narrow * c:(narrow + 1) * c]
                narrow += 1
            el_ref[(ci * n_lvl + l) * c:(ci * n_lvl + l + 1) * c, :] = jnp.exp2(expo).astype(BF16)

    def stage_scores(ci):
        rows = slice(ci * c, (ci + 1) * c)
        qb, kb = qs_ref[rows, :], ks_ref[rows, :]
        lhs, rhs = [qb], [kb]
        for l in range(n_lvl):
            el = el_ref[(ci * n_lvl + l) * c:(ci * n_lvl + l + 1) * c, :]
            lhs.append(qb * el)
            rhs.append(kb * el)
        s = None
        for p in range(m_ref.shape[0]):
            a0, b0 = lhs[2 * p], rhs[2 * p]
            if 2 * p + 1 < len(lhs):
                a = jnp.concatenate([a0, lhs[2 * p + 1]], axis=1)
                b = jnp.concatenate([jnp.concatenate([b0, zeros], axis=1),
                                     jnp.concatenate([zeros, rhs[2 * p + 1]], axis=1)], axis=0)
            else:
                a, b = a0, jnp.concatenate([b0, zeros], axis=0)
            tile = m_ref[p] * _dot_nt(a, b)
            s = tile if s is None else s + tile
        s_ref[rows, :] = s.astype(BF16)

    def stage_intra(ci):
        rows = slice(ci * c, (ci + 1) * c)
        vc = i_ref[rows, :]
        ob_ref[rows, :] = _dot(s_ref[rows, :], jnp.concatenate([vc, vc], axis=0))
        kv_ref[ci] = _dot_tn(vc, kd_ref[rows, :])

    def stage_state(ci):
        rows = slice(ci * c, (ci + 1) * c)
        st = state[0]
        o = ob_ref[rows, :] + _dot_nt(qe_ref[rows, :], st.astype(BF16))
        state[0] = st * dec_ref[ci] + kv_ref[ci]
        o = o * lax.rsqrt(jnp.mean(o * o, axis=-1, keepdims=True) + EPS)
        o = o * nw_ref[...] * _silu(g_ref[rows, :].astype(F32))
        o_ref[rows, :] = (_sigmoid(gm_ref[rows, :].astype(F32)) * o).astype(o_ref.dtype)

    stages = (stage_gates, stage_scores, stage_intra, stage_state)
    for step in range(n_chunks + len(stages) - 1):
        for depth, stage in enumerate(stages):
            if 0 <= step - depth < n_chunks:
                stage(step - depth)
    st_ref[...] = state[0]


def _hgrn2(z, lb, hg_norm_w, batch, seq, t_blk=1024):
    n = batch * seq
    d = hg_norm_w.shape[0]
    dk = d // HG_HEADS
    nt = seq // t_blk
    wmat, mask = _hgrn_level_tables()

    def part(p):
        return pl.BlockSpec((None, t_blk, LANES), lambda b, h, t, p=p: (p * HG_HEADS + h, b * nt + t, 0))

    head_vec = pl.BlockSpec((1, dk), lambda b, h, t: (0, h))
    return pl.pallas_call(
        _hgrn_kernel,
        out_shape=jax.ShapeDtypeStruct((d // LANES, n, LANES), BF16),
        grid=(batch, HG_HEADS, nt),
        in_specs=[
            part(P_HQ), part(P_HF), part(P_HI), part(P_HG), part(P_GH), head_vec, head_vec,
            pl.BlockSpec(wmat.shape, lambda b, h, t: (0, 0)),
            pl.BlockSpec(mask.shape, lambda b, h, t: (0, 0, 0)),
        ],
        out_specs=pl.BlockSpec((None, t_blk, LANES), lambda b, h, t: (h, b * nt + t, 0)),
        scratch_shapes=[
            pltpu.VMEM((dk, dk), F32), pltpu.VMEM((t_blk, dk), BF16), pltpu.VMEM((t_blk, dk), BF16),
            pltpu.VMEM((t_blk, dk), F32),
            pltpu.VMEM((t_blk, dk), BF16), pltpu.VMEM((t_blk, dk), BF16),
            pltpu.VMEM((t_blk // CHUNK, 1, dk), F32),
            pltpu.VMEM((t_blk * len(HG_LEVELS), dk), BF16), pltpu.VMEM((t_blk, 2 * CHUNK), BF16),
            pltpu.VMEM((t_blk // CHUNK, dk, dk), F32),
        ],
        compiler_params=pltpu.CompilerParams(
            dimension_semantics=("parallel", "parallel", "arbitrary"), vmem_limit_bytes=VMEM_LIMIT),
        name="hgrn2",
    )(z, z, z, z, z, lb.reshape(1, d), hg_norm_w.reshape(1, d), wmat, mask)


def _outproj_kernel(a_ref, b_ref, x_ref, w_ref, nw_ref, wr_ref, br_ref, h_ref, hn_ref, gate_ref, idx_ref):
    merged = jnp.concatenate([(a_ref[g].astype(F32) + b_ref[g].astype(F32)).astype(BF16)
                              for g in range(a_ref.shape[0])], axis=1)
    h = x_ref[...] + _dot(merged, w_ref[...])
    h_ref[...] = h
    hn = h * lax.rsqrt(jnp.mean(h * h, axis=-1, keepdims=True) + EPS) * nw_ref[...]
    hn_ref[...] = hn
    n_e = br_ref.shape[-1]
    hn_hi, hn_lo = _split2(hn)
    l2 = _dot(hn_hi, wr_ref[...])
    logits = l2[:, :n_e] + l2[:, n_e:] + _dot(hn_lo, wr_ref[:, :n_e]) + br_ref[...]
    lane = lax.broadcasted_iota(jnp.int32, logits.shape, 1).astype(F32)
    vals, idxs = [], []
    for _ in range(TOP_K):
        m = jnp.max(logits, axis=-1, keepdims=True)
        idx = jnp.min(jnp.where(logits == m, lane, float(n_e)), axis=-1, keepdims=True)
        vals.append(m)
        idxs.append(idx)
        logits = jnp.where(lane == idx, -jnp.inf, logits)
    exps = [jnp.exp(v - vals[0]) for v in vals]
    denom = exps[0] + exps[1] + exps[2] + exps[3]
    for k in range(TOP_K):
        gate_ref[:, k:k + 1] = exps[k] / denom
        idx_ref[:, k:k + 1] = idxs[k].astype(jnp.int32)


def _outproj(o_r, o_h, x2, w_bf, norm_w, w_router, b_router, tm=512):
    n, d = x2.shape
    n_e = w_router.shape[1]
    row = lambda w: pl.BlockSpec((tm, w), lambda i: (i, 0))
    full = lambda a: pl.BlockSpec(a.shape, lambda i: (0, 0))
    slabs = pl.BlockSpec((d // LANES, tm, LANES), lambda i: (0, i, 0))
    nw, br = norm_w.reshape(1, d), b_router.reshape(1, n_e)
    wr = jnp.concatenate(_split2(w_router), axis=1)
    return pl.pallas_call(
        _outproj_kernel,
        out_shape=(jax.ShapeDtypeStruct((n, d), F32), jax.ShapeDtypeStruct((n, d), F32),
                   jax.ShapeDtypeStruct((n, TOP_K), F32), jax.ShapeDtypeStruct((n, TOP_K), jnp.int32)),
        grid=(n // tm,),
        in_specs=[slabs, slabs, row(d), full(w_bf), full(nw), full(wr), full(br)],
        out_specs=(row(d), row(d), row(TOP_K), row(TOP_K)),
        compiler_params=pltpu.CompilerParams(
            dimension_semantics=("parallel",), vmem_limit_bytes=VMEM_LIMIT),
        name="outproj_router",
    )(o_r, o_h, x2, w_bf, nw, wr, br)


def _rank_kernel(idx_ref, tri_ref, rank_ref, cnt_ref, run_ref):
    @pl.when(pl.program_id(0) == 0)
    def _():
        run_ref[...] = jnp.zeros_like(run_ref)

    idx = idx_ref[...]
    lane = lax.broadcasted_iota(jnp.int32, (idx.shape[0], N_EXPERTS), 1)
    run = run_ref[...]
    for k in range(TOP_K):
        onehot = lane == idx[:, k:k + 1]
        before = _dot(tri_ref[...], onehot.astype(BF16)) + run
        rank_ref[:, k:k + 1] = jnp.sum(jnp.where(onehot, before, 0.0), axis=-1, keepdims=True).astype(jnp.int32)
        run = run + jnp.sum(onehot.astype(F32), axis=0, keepdims=True)
    run_ref[...] = run
    cnt_ref[...] = run.astype(jnp.int32)


def _pair_ranks(idx, blk=512):
    n = idx.shape[0]
    r = np.arange(blk)
    tri = jnp.asarray(r[None, :] < r[:, None], BF16)
    return pl.pallas_call(
        _rank_kernel,
        out_shape=(jax.ShapeDtypeStruct((n, TOP_K), jnp.int32), jax.ShapeDtypeStruct((1, N_EXPERTS), jnp.int32)),
        grid=(n // blk,),
        in_specs=[pl.BlockSpec((blk, TOP_K), lambda i: (i, 0)), pl.BlockSpec((blk, blk), lambda i: (0, 0))],
        out_specs=(pl.BlockSpec((blk, TOP_K), lambda i: (i, 0)), pl.BlockSpec((1, N_EXPERTS), lambda i: (0, 0))),
        scratch_shapes=[pltpu.VMEM((1, N_EXPERTS), F32)],
        compiler_params=pltpu.CompilerParams(dimension_semantics=("arbitrary",)),
        name="pair_ranks",
    )(idx, tri)


def _dispatch_kernel(dest_ref, src_ref, dst_ref, sem, *, tc):
    base = pl.program_id(0) * tc * TOP_K

    def issue(t, carry):
        for k in range(TOP_K):
            pltpu.make_async_copy(src_ref.at[pl.ds(t, 1)],
                                  dst_ref.at[pl.ds(dest_ref[base + t * TOP_K + k], 1)],
                                  sem).start(priority=k % 2)
        return carry

    lax.fori_loop(0, tc, issue, 0)
    for _ in range(TOP_K):
        pltpu.make_async_copy(src_ref, dst_ref.at[pl.ds(0, tc)], sem).wait()


def _dispatch_rows(src, dest, m_pad, tc=256):
    n, d = src.shape
    return pl.pallas_call(
        functools.partial(_dispatch_kernel, tc=tc),
        out_shape=jax.ShapeDtypeStruct((m_pad, d), src.dtype),
        grid_spec=pltpu.PrefetchScalarGridSpec(
            num_scalar_prefetch=1, grid=(n // tc,),
            in_specs=[pl.BlockSpec((tc, d), lambda i, dref: (i, 0))],
            out_specs=pl.BlockSpec(memory_space=pl.ANY),
            scratch_shapes=[pltpu.SemaphoreType.DMA]),
        compiler_params=pltpu.CompilerParams(
            dimension_semantics=("arbitrary",), vmem_limit_bytes=VMEM_LIMIT),
        name="dispatch_rows",
    )(dest, src)


PAIR = 2 * LANES
SUB_ROWS = 256


def _even_lane_selector():
    f = np.arange(LANES)
    sel = np.zeros((PAIR, LANES), np.float32)
    sel[2 * f, f] = 1.0
    return jnp.asarray(sel, BF16)


def _expert_kernel(be_ref, nv_ref, nb_ref, x_ref, w1_ref, b1_ref, w2_ref, b2_ref, sel_ref, y_ref,
                   xb_ref):
    i, j = pl.program_id(0), pl.program_id(1)
    n_valid = nv_ref[i]

    @pl.when(i < nb_ref[0])
    def _():
        @pl.when(j == 0)
        def _():
            row = lax.broadcasted_iota(jnp.int32, (x_ref.shape[0], 1), 0)
            xb_ref[...] = jnp.where(row < n_valid, x_ref[...], 0.0).astype(BF16)

        def rows_pass(start_row, n_rows):
            rows = pl.ds(pl.multiple_of(start_row, min(SUB_ROWS, x_ref.shape[0])), n_rows)
            even = lax.broadcasted_iota(jnp.int32, (n_rows, LANES), 1) % 2 == 0
            hcat = _dot(xb_ref[rows, :], w1_ref[...].astype(BF16)) + b1_ref[...]
            hid = []
            for g in range(hcat.shape[1] // PAIR):
                act = []
                for v in range(PAIR // LANES):
                    h = hcat[:, g * PAIR + v * LANES:g * PAIR + (v + 1) * LANES]
                    glu = jnp.minimum(h, SWIGLU_LIMIT)
                    lin = jnp.clip(pltpu.roll(h, LANES - 1, 1), -SWIGLU_LIMIT, SWIGLU_LIMIT)
                    a = glu * _sigmoid(SWIGLU_ALPHA * glu) * (lin + 1.0)
                    act.append(jnp.where(even, a, 0.0).astype(BF16))
                hid.append(_dot(jnp.concatenate(act, axis=1), sel_ref[...]).astype(BF16))
            part = _dot(jnp.concatenate(hid, axis=1), w2_ref[...].astype(BF16))
            start = jnp.where(j == 0, jnp.broadcast_to(b2_ref[...], part.shape), y_ref[rows, :])
            y_ref[rows, :] = start + part

        unit = min(SUB_ROWS, x_ref.shape[0])
        n_units = (n_valid + unit - 1) // unit
        for bit in reversed(range((x_ref.shape[0] // unit).bit_length())):
            @pl.when(((n_units >> bit) & 1) == 1)
            def _(bit=bit):
                rows_pass((n_units >> (bit + 1) << (bit + 1)) * unit, unit << bit)


def _experts(xs, block_expert, block_valid, n_blocks_used, w1, b1, w2, b2, tm, tf=256):
    m_pad, d = xs.shape
    f = w2.shape[1]
    tf = min(tf, f)
    nf = f // tf
    nblk = m_pad // tm
    sel = _even_lane_selector()

    def blk(i, nb):
        return jnp.minimum(i, nb[0] - 1)

    def ftile(i, j, nb):
        return jnp.where(i < nb[0], j, nf - 1)

    return pl.pallas_call(
        _expert_kernel,
        out_shape=jax.ShapeDtypeStruct((m_pad, d), F32),
        grid_spec=pltpu.PrefetchScalarGridSpec(
            num_scalar_prefetch=3, grid=(nblk, nf),
            in_specs=[
                pl.BlockSpec((tm, d), lambda i, j, be, nv, nb: (blk(i, nb), 0)),
                pl.BlockSpec((None, d, 2 * tf), lambda i, j, be, nv, nb: (be[blk(i, nb)], 0, ftile(i, j, nb))),
                pl.BlockSpec((None, 1, 2 * tf), lambda i, j, be, nv, nb: (be[blk(i, nb)], 0, ftile(i, j, nb))),
                pl.BlockSpec((None, tf, d), lambda i, j, be, nv, nb: (be[blk(i, nb)], ftile(i, j, nb), 0)),
                pl.BlockSpec((None, 1, d), lambda i, j, be, nv, nb: (be[blk(i, nb)], 0, 0)),
                pl.BlockSpec(sel.shape, lambda i, j, be, nv, nb: (0, 0)),
            ],
            out_specs=pl.BlockSpec((tm, d), lambda i, j, be, nv, nb: (blk(i, nb), 0)),
            scratch_shapes=[pltpu.VMEM((tm, d), BF16)]),
        compiler_params=pltpu.CompilerParams(
            dimension_semantics=("arbitrary", "arbitrary"), vmem_limit_bytes=VMEM_LIMIT),
        name="experts",
    )(block_expert, block_valid, n_blocks_used, xs, w1, b1, w2, b2, sel)


def _combine_kernel(dest_ref, h_ref, gate_ref, nw_ref, y_ref, o_ref, buf_ref, sem, *, tc):
    i = pl.program_id(0)
    slot = i % 2

    def fetch(block, into):
        base = block * tc * TOP_K

        def issue(t, carry):
            for k in range(TOP_K):
                pltpu.make_async_copy(y_ref.at[pl.ds(dest_ref[base + t * TOP_K + k], 1)],
                                      buf_ref.at[into, k, pl.ds(t, 1)], sem.at[into]).start(priority=k % 2)
            return carry

        lax.fori_loop(0, tc, issue, 0)

    @pl.when(i == 0)
    def _():
        fetch(0, 0)

    @pl.when(i + 1 < pl.num_programs(0))
    def _():
        fetch(i + 1, 1 - slot)

    for k in range(TOP_K):
        pltpu.make_async_copy(y_ref.at[pl.ds(0, tc)], buf_ref.at[slot, k], sem.at[slot]).wait()
    gate = gate_ref[...]
    out = h_ref[...]
    for k in range(TOP_K):
        out = out + gate[:, k:k + 1] * buf_ref[slot, k]
    out = out * lax.rsqrt(jnp.mean(out * out, axis=-1, keepdims=True) + EPS) * nw_ref[...]
    o_ref[...] = out


def _combine(dest, h, gates, norm_w, y, tc=256):
    n, d = h.shape
    return pl.pallas_call(
        functools.partial(_combine_kernel, tc=tc),
        out_shape=jax.ShapeDtypeStruct((n, d), F32),
        grid_spec=pltpu.PrefetchScalarGridSpec(
            num_scalar_prefetch=1, grid=(n // tc,),
            in_specs=[
                pl.BlockSpec((tc, d), lambda i, dref: (i, 0)),
                pl.BlockSpec((tc, TOP_K), lambda i, dref: (i, 0)),
                pl.BlockSpec((1, d), lambda i, dref: (0, 0)),
                pl.BlockSpec(memory_space=pl.ANY),
            ],
            out_specs=pl.BlockSpec((tc, d), lambda i, dref: (i, 0)),
            scratch_shapes=[pltpu.VMEM((2, TOP_K, tc, d), F32), pltpu.SemaphoreType.DMA((2,))]),
        compiler_params=pltpu.CompilerParams(
            dimension_semantics=("arbitrary",), vmem_limit_bytes=VMEM_LIMIT),
        name="combine_norm",
    )(dest, h, gates, norm_w.reshape(1, d), y)


def _moe(h, hn, gates, idx, w1, b1, w2, b2, norm_final_w, tm=1024, tf=256):
    n, d = h.shape
    n_pairs = n * TOP_K
    e_flat = idx.reshape(n_pairs)
    rank, counts = _pair_ranks(idx, blk=min(512, n))
    counts = counts.reshape(N_EXPERTS)
    padded = (counts + tm - 1) // tm * tm
    end_pad = jnp.cumsum(padded)
    start_pad = end_pad - padded
    dest = start_pad[e_flat] + rank.reshape(n_pairs)
    dest = dest.astype(jnp.int32)
    m_pad = n_pairs + N_EXPERTS * tm
    n_blocks = m_pad // tm
    block_start = jnp.arange(n_blocks, dtype=jnp.int32) * tm
    block_expert = jnp.minimum(jnp.sum(end_pad[None, :] <= block_start[:, None], axis=1),
                               N_EXPERTS - 1).astype(jnp.int32)
    block_valid = jnp.clip((start_pad + counts)[block_expert] - block_start, 0, tm).astype(jnp.int32)
    n_blocks_used = (end_pad[-1:] // tm).astype(jnp.int32)
    xs = _dispatch_rows(hn, dest, m_pad, tc=min(512, n))
    y = _experts(xs, block_expert, block_valid, n_blocks_used, w1, b1[:, None, :], w2, b2[:, None, :], tm, tf)
    return _combine(dest, h, gates, norm_final_w, y)


def kernel(x, w_in, ret_norm_w, hg_norm_w, hg_lb_logits, w_out, norm_mix_w, norm_moe_w, w_router, b_router,
           w1, b1, w2, b2, norm_final_w):
    batch, seq, d = x.shape
    depth = w_in.shape[0]
    assert depth == 1, "the final norm is fused into the last layer's combine; one layer supported"
    lb_all = jnp.cumsum(jax.nn.softmax(hg_lb_logits.astype(F32), axis=0), axis=0)
    x2 = x.reshape(batch * seq, d)
    z = _inproj(x2, norm_mix_w[0], w_in[0].astype(BF16))
    o_r = _retention(z, ret_norm_w[0], batch, seq)
    o_h = _hgrn2(z, lb_all[0], hg_norm_w[0], batch, seq)
    h, hn, gates, idx = _outproj(o_r, o_h, x2, w_out[0].astype(BF16), norm_moe_w[0], w_router[0], b_router[0])
    out = _moe(h, hn, gates, idx, w1[0], b1[0], w2[0], b2[0], norm_final_w)
    return out.reshape(batch, seq, d)
```

```python
import functools

import numpy as np
import jax
import jax.numpy as jnp
from jax import lax
from jax.experimental import pallas as pl
from jax.experimental.pallas import tpu as pltpu

F32 = jnp.float32
BF16 = jnp.bfloat16

LANES = 128
SUBLANES = 8
CHUNK = 64
HG_LEVELS = (32, 16, 8, 4, 2, 1)
RET_HEADS = 8
HG_HEADS = 16
N_SPLITS = 10
N_EXPERTS = 32
TOP_K = 4
SWIGLU_ALPHA = 1.702
SWIGLU_LIMIT = 7.0
ROPE_BASE = 10000.0
EPS = 1e-6
LOG2_E = 1.4426950408889634
VMEM_LIMIT = 56 * 1024 * 1024

P_RQ, P_RK, P_RV, P_RG, P_HQ, P_HF, P_HI, P_HG, P_GR, P_GH = range(N_SPLITS)


def _sigmoid(x):
    return 1.0 / (1.0 + jnp.exp(-x))


def _sigmoid_pair(x):
    t = jnp.exp(-jnp.abs(x))
    r = 1.0 / (1.0 + t)
    p = t * r
    pos = x >= 0.0
    return jnp.where(pos, r, p), jnp.where(pos, p, r)


def _silu(x):
    return x * _sigmoid(x)


def _dot(a, b):
    return jnp.dot(a, b, preferred_element_type=F32)


def _dot_nt(a, b):
    return lax.dot_general(a, b, (((1,), (1,)), ((), ())), preferred_element_type=F32)


def _dot_tn(a, b):
    return lax.dot_general(a, b, (((0,), (0,)), ((), ())), preferred_element_type=F32)


def _inproj_kernel(x_ref, nw_ref, w_ref, o_ref, xn_ref):
    @pl.when(pl.program_id(1) == 0)
    def _():
        x = x_ref[...]
        ms = jnp.mean(x * x, axis=-1, keepdims=True)
        xn_ref[...] = (x * lax.rsqrt(ms + EPS) * nw_ref[...]).astype(BF16)

    acc = _dot(xn_ref[...], w_ref[...])
    for g in range(o_ref.shape[0]):
        o_ref[g] = acc[:, g * LANES:(g + 1) * LANES].astype(o_ref.dtype)


def _inproj(x2, norm_w, w_bf, tm=1024, tn=1024):
    n, d = x2.shape
    cols = w_bf.shape[1]
    gpt = tn // LANES
    return pl.pallas_call(
        _inproj_kernel,
        out_shape=jax.ShapeDtypeStruct((cols // LANES, n, LANES), BF16),
        grid=(n // tm, cols // tn),
        in_specs=[
            pl.BlockSpec((tm, d), lambda i, j: (i, 0)),
            pl.BlockSpec((1, d), lambda i, j: (0, 0)),
            pl.BlockSpec((d, tn), lambda i, j: (0, j)),
        ],
        out_specs=pl.BlockSpec((gpt, tm, LANES), lambda i, j: (j, i, 0)),
        scratch_shapes=[pltpu.VMEM((tm, d), BF16)],
        compiler_params=pltpu.CompilerParams(
            dimension_semantics=("parallel", "arbitrary"), vmem_limit_bytes=VMEM_LIMIT),
        name="inproj",
    )(x2, norm_w.reshape(1, d), w_bf)


def _retention_tables(seq, t_blk, dk):
    h = RET_HEADS
    log_gamma = jnp.log1p(-jnp.exp2(-5.0 - jnp.arange(h, dtype=F32)))
    inv = 1.0 / (ROPE_BASE ** jnp.linspace(0.0, 1.0, dk // 2, dtype=F32))
    ang = jnp.arange(seq, dtype=F32)[:, None] * inv[None, :]
    cos, sin = jnp.cos(ang), jnp.sin(ang)
    pos = np.arange(t_blk)
    cq, ck = pos[:, None] // CHUNK, pos[None, :] // CHUNK
    dist = (pos[:, None] - pos[None, :]).astype(np.float32)
    expo = np.where(cq == ck, np.abs(dist), dist)
    allowed = jnp.asarray(cq >= ck)
    dmask = jnp.where(allowed[None], jnp.exp(jnp.asarray(expo)[None] * log_gamma[:, None, None]), 0.0)
    posf = jnp.arange(t_blk, dtype=F32)
    qdec = jnp.exp((posf[None, :] + 1.0) * log_gamma[:, None])[..., None]
    kdec = jnp.exp((t_blk - 1.0 - posf)[None, :] * log_gamma[:, None])[..., None]
    sdec = jnp.broadcast_to(jnp.exp(t_blk * log_gamma)[:, None, None], (h, 1, LANES))
    return cos, sin, dmask, qdec, kdec, sdec


def _retention_kernel(q_ref, k_ref, v_ref, g_ref, gm_ref, cos_ref, sin_ref, dm_ref, qd_ref, kd_ref,
                      sd_ref, nw_ref, o_ref, st_ref, *, scale):
    @pl.when(pl.program_id(2) == 0)
    def _():
        st_ref[...] = jnp.zeros_like(st_ref)

    cos, sin = cos_ref[...], sin_ref[...]

    def rot(ref):
        x1, x2 = ref[0].astype(F32), ref[1].astype(F32)
        return jnp.concatenate([x1 * cos - x2 * sin, x2 * cos + x1 * sin], axis=-1)

    q = rot(q_ref)
    k = rot(k_ref) * scale
    v = jnp.concatenate([v_ref[0], v_ref[1]], axis=-1)
    q_bf, k_bf = q.astype(BF16), k.astype(BF16)
    scores = _dot_nt(q_bf, k_bf) * dm_ref[...]
    state = st_ref[...]
    o = _dot(scores.astype(BF16), v) + _dot((q * qd_ref[...]).astype(BF16), state.astype(BF16))
    st_ref[...] = sd_ref[:, 0:1] * state + _dot_tn((k * kd_ref[...]).astype(BF16), v)

    o = o - jnp.mean(o, axis=-1, keepdims=True)
    o = o * lax.rsqrt(jnp.mean(o * o, axis=-1, keepdims=True) + EPS)
    gate = jnp.concatenate([g_ref[0], g_ref[1]], axis=-1).astype(F32)
    merge = jnp.concatenate([gm_ref[0], gm_ref[1]], axis=-1).astype(F32)
    res = (_sigmoid(merge) * (o * nw_ref[...] * _silu(gate))).astype(o_ref.dtype)
    for g in range(o_ref.shape[0]):
        o_ref[g] = res[:, g * LANES:(g + 1) * LANES]


def _retention(z, ret_norm_w, batch, seq, t_blk=512):
    n = batch * seq
    d = ret_norm_w.shape[0]
    dk = d // RET_HEADS
    gph = dk // LANES
    nt = seq // t_blk
    cos, sin, dmask, qdec, kdec, sdec = _retention_tables(seq, t_blk, dk)

    def part(p):
        return pl.BlockSpec((gph, t_blk, LANES), lambda b, h, t, p=p: (p * RET_HEADS + h, b * nt + t, 0))

    per_head = lambda shape: pl.BlockSpec((None,) + shape, lambda b, h, t: (h, 0, 0))
    return pl.pallas_call(
        functools.partial(_retention_kernel, scale=dk ** -0.5),
        out_shape=jax.ShapeDtypeStruct((d // LANES, n, LANES), BF16),
        grid=(batch, RET_HEADS, nt),
        in_specs=[
            part(P_RQ), part(P_RK), part(P_RV), part(P_RG), part(P_GR),
            pl.BlockSpec((t_blk, dk // 2), lambda b, h, t: (t, 0)),
            pl.BlockSpec((t_blk, dk // 2), lambda b, h, t: (t, 0)),
            per_head((t_blk, t_blk)), per_head((t_blk, 1)), per_head((t_blk, 1)), per_head((1, LANES)),
            pl.BlockSpec((1, dk), lambda b, h, t: (0, h)),
        ],
        out_specs=pl.BlockSpec((gph, t_blk, LANES), lambda b, h, t: (h, b * nt + t, 0)),
        scratch_shapes=[pltpu.VMEM((dk, dk), F32)],
        compiler_params=pltpu.CompilerParams(
            dimension_semantics=("parallel", "parallel", "arbitrary"), vmem_limit_bytes=VMEM_LIMIT),
        name="retention",
    )(z, z, z, z, z, cos, sin, dmask, qdec, kdec, sdec, ret_norm_w.reshape(1, d))


def _hgrn_level_tables():
    c = CHUNK
    r = np.arange(c)
    mats = [(r[None, :] <= r[:, None])]
    masks = [np.eye(c, dtype=bool)]
    for s in HG_LEVELS:
        blk, within = r // (2 * s), r % (2 * s)
        mid = blk * 2 * s + s - 1
        right = within >= s
        t = r[None, :]
        if s < SUBLANES:
            mats.append(np.where(right[:, None], (t > mid[:, None]) & (t <= r[:, None]),
                                 (t > r[:, None]) & (t <= mid[:, None])))
        masks.append((blk[:, None] == blk[None, :]) & right[:, None] & (~right)[None, :])
    wmat = np.concatenate(mats, axis=0).astype(np.float32)
    if len(masks) % 2:
        masks.append(np.zeros((c, c), bool))
    mask = np.stack([np.concatenate(masks[p:p + 2], axis=1) for p in range(0, len(masks), 2)])
    return jnp.asarray(wmat, BF16), jnp.asarray(mask.astype(np.float32), F32)


def _split2(x):
    hi = x.astype(BF16)
    return hi, (x - hi.astype(F32)).astype(BF16)


def _level_exponent(cum, s):
    pieces = []
    for r0 in range(0, CHUNK, SUBLANES):
        mid = r0 // (2 * s) * 2 * s + s - 1
        rows, ref = cum[r0:r0 + SUBLANES], cum[mid:mid + 1]
        pieces.append(rows - ref if r0 % (2 * s) >= s else ref - rows)
    return jnp.concatenate(pieces, axis=0)


def _hgrn_kernel(q_ref, f_ref, i_ref, g_ref, gm_ref, lb_ref, nw_ref, w_ref, m_ref, o_ref,
                 st_ref, qs_ref, ks_ref, ob_ref, qe_ref, kd_ref, dec_ref, el_ref, s_ref, kv_ref):
    @pl.when(pl.program_id(2) == 0)
    def _():
        st_ref[...] = jnp.zeros_like(st_ref)

    c = CHUNK
    t_blk, dk = qs_ref.shape
    n_lvl = len(HG_LEVELS)
    n_chunks = t_blk // c
    lb = lb_ref[...]
    zeros = jnp.zeros((c, dk), BF16)
    state = [st_ref[...]]

    def stage_gates(ci):
        rows = slice(ci * c, (ci + 1) * c)
        z = f_ref[rows, :].astype(F32)
        qs = _silu(q_ref[rows, :].astype(F32))
        sig_pos, sig_neg = _sigmoid_pair(z)
        ks = (1.0 - lb) * sig_neg
        qs_ref[rows, :] = qs.astype(BF16)
        ks_ref[rows, :] = ks.astype(BF16)
        hi, lo = _split2(jnp.log(lb + (1.0 - lb) * sig_pos) * LOG2_E)
        g2 = _dot(w_ref[...], jnp.concatenate([hi, lo], axis=-1))
        g = g2[:, :dk] + g2[:, dk:]
        cum = g[0:c]
        e_start = jnp.exp2(cum)
        qe_ref[rows, :] = (qs * e_start).astype(BF16)
        kd_ref[rows, :] = (ks * jnp.exp2(cum[c - 1:c] - cum)).astype(BF16)
        dec_ref[ci] = e_start[c - 1:c]
        narrow = 1
        for l, s in enumerate(HG_LEVELS):
            if s >= SUBLANES:
                expo = _level_exponent(cum, s)
            else:
                expo = g[narrow * c:(narrow + 1) * c]
                narrow += 1
            el_ref[(ci * n_lvl + l) * c:(ci * n_lvl + l + 1) * c, :] = jnp.exp2(expo).astype(BF16)

    def stage_scores(ci):
        rows = slice(ci * c, (ci + 1) * c)
        qb, kb = qs_ref[rows, :], ks_ref[rows, :]
        lhs, rhs = [qb], [kb]
        for l in range(n_lvl):
            el = el_ref[(ci * n_lvl + l) * c:(ci * n_lvl + l + 1) * c, :]
            lhs.append(qb * el)
            rhs.append(kb * el)
        s = None
        for p in range(m_ref.shape[0]):
            a0, b0 = lhs[2 * p], rhs[2 * p]
            if 2 * p + 1 < len(lhs):
                a = jnp.concatenate([a0, lhs[2 * p + 1]], axis=1)
                b = jnp.concatenate([jnp.concatenate([b0, zeros], axis=1),
                                     jnp.concatenate([zeros, rhs[2 * p + 1]], axis=1)], axis=0)
            else:
                a, b = a0, jnp.concatenate([b0, zeros], axis=0)
            tile = m_ref[p] * _dot_nt(a, b)
            s = tile if s is None else s + tile
        s_ref[rows, :] = s.astype(BF16)

    def stage_intra(ci):
        rows = slice(ci * c, (ci + 1) * c)
        vc = i_ref[rows, :]
        ob_ref[rows, :] = _dot(s_ref[rows, :], jnp.concatenate([vc, vc], axis=0))
        kv_ref[ci] = _dot_tn(vc, kd_ref[rows, :])

    def stage_state(ci):
        rows = slice(ci * c, (ci + 1) * c)
        st = state[0]
        o = ob_ref[rows, :] + _dot_nt(qe_ref[rows, :], st.astype(BF16))
        state[0] = st * dec_ref[ci] + kv_ref[ci]
        o = o * lax.rsqrt(jnp.mean(o * o, axis=-1, keepdims=True) + EPS)
        o = o * nw_ref[...] * _silu(g_ref[rows, :].astype(F32))
        o_ref[rows, :] = (_sigmoid(gm_ref[rows, :].astype(F32)) * o).astype(o_ref.dtype)

    stages = (stage_gates, stage_scores, stage_intra, stage_state)
    for step in range(n_chunks + len(stages) - 1):
        for depth, stage in enumerate(stages):
            if 0 <= step - depth < n_chunks:
                stage(step - depth)
    st_ref[...] = state[0]


def _hgrn2(z, lb, hg_norm_w, batch, seq, t_blk=1024):
    n = batch * seq
    d = hg_norm_w.shape[0]
    dk = d // HG_HEADS
    nt = seq // t_blk
    wmat, mask = _hgrn_level_tables()

    def part(p):
        return pl.BlockSpec((None, t_blk, LANES), lambda b, h, t, p=p: (p * HG_HEADS + h, b * nt + t, 0))

    head_vec = pl.BlockSpec((1, dk), lambda b, h, t: (0, h))
    return pl.pallas_call(
        _hgrn_kernel,
        out_shape=jax.ShapeDtypeStruct((d // LANES, n, LANES), BF16),
        grid=(batch, HG_HEADS, nt),
        in_specs=[
            part(P_HQ), part(P_HF), part(P_HI), part(P_HG), part(P_GH), head_vec, head_vec,
            pl.BlockSpec(wmat.shape, lambda b, h, t: (0, 0)),
            pl.BlockSpec(mask.shape, lambda b, h, t: (0, 0, 0)),
        ],
        out_specs=pl.BlockSpec((None, t_blk, LANES), lambda b, h, t: (h, b * nt + t, 0)),
        scratch_shapes=[
            pltpu.VMEM((dk, dk), F32), pltpu.VMEM((t_blk, dk), BF16), pltpu.VMEM((t_blk, dk), BF16),
            pltpu.VMEM((t_blk, dk), F32),
            pltpu.VMEM((t_blk, dk), BF16), pltpu.VMEM((t_blk, dk), BF16),
            pltpu.VMEM((t_blk // CHUNK, 1, dk), F32),
            pltpu.VMEM((t_blk * len(HG_LEVELS), dk), BF16), pltpu.VMEM((t_blk, 2 * CHUNK), BF16),
            pltpu.VMEM((t_blk // CHUNK, dk, dk), F32),
        ],
        compiler_params=pltpu.CompilerParams(
            dimension_semantics=("parallel", "parallel", "arbitrary"), vmem_limit_bytes=VMEM_LIMIT),
        name="hgrn2",
    )(z, z, z, z, z, lb.reshape(1, d), hg_norm_w.reshape(1, d), wmat, mask)


def _outproj_kernel(a_ref, b_ref, x_ref, w_ref, nw_ref, wr_ref, br_ref, h_ref, hn_ref, gate_ref, idx_ref):
    merged = jnp.concatenate([(a_ref[g].astype(F32) + b_ref[g].astype(F32)).astype(BF16)
                              for g in range(a_ref.shape[0])], axis=1)
    h = x_ref[...] + _dot(merged, w_ref[...])
    h_ref[...] = h
    hn = h * lax.rsqrt(jnp.mean(h * h, axis=-1, keepdims=True) + EPS) * nw_ref[...]
    hn_ref[...] = hn
    n_e = br_ref.shape[-1]
    hn_hi, hn_lo = _split2(hn)
    l2 = _dot(hn_hi, wr_ref[...])
    logits = l2[:, :n_e] + l2[:, n_e:] + _dot(hn_lo, wr_ref[:, :n_e]) + br_ref[...]
    lane = lax.broadcasted_iota(jnp.int32, logits.shape, 1).astype(F32)
    vals, idxs = [], []
    for _ in range(TOP_K):
        m = jnp.max(logits, axis=-1, keepdims=True)
        idx = jnp.min(jnp.where(logits == m, lane, float(n_e)), axis=-1, keepdims=True)
        vals.append(m)
        idxs.append(idx)
        logits = jnp.where(lane == idx, -jnp.inf, logits)
    exps = [jnp.exp(v - vals[0]) for v in vals]
    denom = exps[0] + exps[1] + exps[2] + exps[3]
    for k in range(TOP_K):
        gate_ref[:, k:k + 1] = exps[k] / denom
        idx_ref[:, k:k + 1] = idxs[k].astype(jnp.int32)


def _outproj(o_r, o_h, x2, w_bf, norm_w, w_router, b_router, tm=512):
    n, d = x2.shape
    n_e = w_router.shape[1]
    row = lambda w: pl.BlockSpec((tm, w), lambda i: (i, 0))
    full = lambda a: pl.BlockSpec(a.shape, lambda i: (0, 0))
    slabs = pl.BlockSpec((d // LANES, tm, LANES), lambda i: (0, i, 0))
    nw, br = norm_w.reshape(1, d), b_router.reshape(1, n_e)
    wr = jnp.concatenate(_split2(w_router), axis=1)
    return pl.pallas_call(
        _outproj_kernel,
        out_shape=(jax.ShapeDtypeStruct((n, d), F32), jax.ShapeDtypeStruct((n, d), F32),
                   jax.ShapeDtypeStruct((n, TOP_K), F32), jax.ShapeDtypeStruct((n, TOP_K), jnp.int32)),
        grid=(n // tm,),
        in_specs=[slabs, slabs, row(d), full(w_bf), full(nw), full(wr), full(br)],
        out_specs=(row(d), row(d), row(TOP_K), row(TOP_K)),
        compiler_params=pltpu.CompilerParams(
            dimension_semantics=("parallel",), vmem_limit_bytes=VMEM_LIMIT),
        name="outproj_router",
    )(o_r, o_h, x2, w_bf, nw, wr, br)


def _rank_kernel(idx_ref, tri_ref, rank_ref, cnt_ref, run_ref):
    @pl.when(pl.program_id(0) == 0)
    def _():
        run_ref[...] = jnp.zeros_like(run_ref)

    idx = idx_ref[...]
    lane = lax.broadcasted_iota(jnp.int32, (idx.shape[0], N_EXPERTS), 1)
    run = run_ref[...]
    for k in range(TOP_K):
        onehot = lane == idx[:, k:k + 1]
        before = _dot(tri_ref[...], onehot.astype(BF16)) + run
        rank_ref[:, k:k + 1] = jnp.sum(jnp.where(onehot, before, 0.0), axis=-1, keepdims=True).astype(jnp.int32)
        run = run + jnp.sum(onehot.astype(F32), axis=0, keepdims=True)
    run_ref[...] = run
    cnt_ref[...] = run.astype(jnp.int32)


def _pair_ranks(idx, blk=512):
    n = idx.shape[0]
    r = np.arange(blk)
    tri = jnp.asarray(r[None, :] < r[:, None], BF16)
    return pl.pallas_call(
        _rank_kernel,
        out_shape=(jax.ShapeDtypeStruct((n, TOP_K), jnp.int32), jax.ShapeDtypeStruct((1, N_EXPERTS), jnp.int32)),
        grid=(n // blk,),
        in_specs=[pl.BlockSpec((blk, TOP_K), lambda i: (i, 0)), pl.BlockSpec((blk, blk), lambda i: (0, 0))],
        out_specs=(pl.BlockSpec((blk, TOP_K), lambda i: (i, 0)), pl.BlockSpec((1, N_EXPERTS), lambda i: (0, 0))),
        scratch_shapes=[pltpu.VMEM((1, N_EXPERTS), F32)],
        compiler_params=pltpu.CompilerParams(dimension_semantics=("arbitrary",)),
        name="pair_ranks",
    )(idx, tri)


def _dispatch_kernel(dest_ref, src_ref, dst_ref, sem, *, tc):
    base = pl.program_id(0) * tc * TOP_K

    def issue(tile, carry):
        src_tile = src_ref.at[pl.ds(pl.multiple_of(tile * SUBLANES, SUBLANES), SUBLANES)]
        for u in range(SUBLANES):
            for k in range(TOP_K):
                pair = base + (tile * SUBLANES + u) * TOP_K + k
                pltpu.make_async_copy(src_tile.at[pl.ds(u, 1)], dst_ref.at[pl.ds(dest_ref[pair], 1)],
                                      sem).start(priority=k % 2)
        return carry

    lax.fori_loop(0, tc // SUBLANES, issue, 0)
    for _ in range(TOP_K):
        pltpu.make_async_copy(src_ref, dst_ref.at[pl.ds(0, tc)], sem).wait()


def _dispatch_rows(src, dest, m_pad, tc=256):
    n, d = src.shape
    return pl.pallas_call(
        functools.partial(_dispatch_kernel, tc=tc),
        out_shape=jax.ShapeDtypeStruct((m_pad, d), src.dtype),
        grid_spec=pltpu.PrefetchScalarGridSpec(
            num_scalar_prefetch=1, grid=(n // tc,),
            in_specs=[pl.BlockSpec((tc, d), lambda i, dref: (i, 0))],
            out_specs=pl.BlockSpec(memory_space=pl.ANY),
            scratch_shapes=[pltpu.SemaphoreType.DMA]),
        compiler_params=pltpu.CompilerParams(
            dimension_semantics=("arbitrary",), vmem_limit_bytes=VMEM_LIMIT),
        name="dispatch_rows",
    )(dest, src)


PAIR = 2 * LANES
SUB_ROWS = 256


def _even_lane_selector():
    c = np.arange(LANES)
    sel = np.zeros((LANES, LANES), np.float32)
    sel[c, c // 2 + LANES // 2 * (c % 2)] = 1.0
    return jnp.asarray(sel, BF16)


def _expert_kernel(be_ref, nv_ref, nb_ref, x_ref, w1_ref, b1_ref, w2_ref, b2_ref, sel_ref, y_ref,
                   xb_ref):
    i, j = pl.program_id(0), pl.program_id(1)
    n_valid = nv_ref[i]
    tm = xb_ref.shape[0]

    @pl.when(i < nb_ref[0])
    def _():
        @pl.when(j == 0)
        def _():
            row = lax.broadcasted_iota(jnp.int32, (tm, 1), 0)
            xb_ref[...] = jnp.where(row < n_valid, x_ref[...], 0.0).astype(BF16)

        def rows_pass(start_row, n_rows):
            align = min(SUB_ROWS, tm)
            halves = 2 if n_rows >= 2 * SUB_ROWS else 1
            hr = n_rows // halves
            even = lax.broadcasted_iota(jnp.int32, (hr, LANES), 1) % 2 == 0
            w1b, w2b = w1_ref[...].astype(BF16), w2_ref[...].astype(BF16)
            hcats = []
            for p in range(halves):
                row0 = pl.multiple_of(start_row + p * hr, align)
                hcats.append((row0, _dot(xb_ref[pl.ds(row0, hr), :], w1b) + b1_ref[...]))
            for row0, hcat in hcats:
                hid = []
                for g in range(hcat.shape[1] // PAIR):
                    h0, h1 = hcat[:, g * PAIR:g * PAIR + LANES], hcat[:, g * PAIR + LANES:(g + 1) * PAIR]
                    glu = jnp.where(even, h0, pltpu.roll(h1, 1, 1))
                    lin = jnp.where(even, pltpu.roll(h0, LANES - 1, 1), h1)
                    glu = jnp.minimum(glu, SWIGLU_LIMIT)
                    lin = jnp.clip(lin, -SWIGLU_LIMIT, SWIGLU_LIMIT)
                    act = (glu * _sigmoid(SWIGLU_ALPHA * glu) * (lin + 1.0)).astype(BF16)
                    hid.append(_dot(act, sel_ref[...]).astype(BF16))
                part = _dot(jnp.concatenate(hid, axis=1), w2b)
                rows = pl.ds(row0, hr)
                start = jnp.where(j == 0, jnp.broadcast_to(b2_ref[...], part.shape), y_ref[rows, :])
                y_ref[rows, :] = start + part

        unit = min(SUB_ROWS, tm)
        n_units = (n_valid + unit - 1) // unit
        for bit in reversed(range((tm // unit).bit_length())):
            @pl.when(((n_units >> bit) & 1) == 1)
            def _(bit=bit):
                rows_pass((n_units >> (bit + 1) << (bit + 1)) * unit, unit << bit)


def _experts(xs, block_expert, block_valid, n_blocks_used, w1, b1, w2, b2, tm, tf=256):
    m_pad, d = xs.shape
    f = w2.shape[1]
    tf = min(tf, f)
    nf = f // tf
    nblk = m_pad // tm
    sel = _even_lane_selector()

    def blk(i, nb):
        return jnp.minimum(i, nb[0] - 1)

    def ftile(i, j, nb):
        return jnp.where(i < nb[0], j, nf - 1)

    return pl.pallas_call(
        _expert_kernel,
        out_shape=jax.ShapeDtypeStruct((m_pad, d), F32),
        grid_spec=pltpu.PrefetchScalarGridSpec(
            num_scalar_prefetch=3, grid=(nblk, nf),
            in_specs=[
                pl.BlockSpec((tm, d), lambda i, j, be, nv, nb: (blk(i, nb), 0)),
                pl.BlockSpec((None, d, 2 * tf), lambda i, j, be, nv, nb: (be[blk(i, nb)], 0, ftile(i, j, nb))),
                pl.BlockSpec((None, 1, 2 * tf), lambda i, j, be, nv, nb: (be[blk(i, nb)], 0, ftile(i, j, nb))),
                pl.BlockSpec((None, tf, d), lambda i, j, be, nv, nb: (be[blk(i, nb)], ftile(i, j, nb), 0)),
                pl.BlockSpec((None, 1, d), lambda i, j, be, nv, nb: (be[blk(i, nb)], 0, 0)),
                pl.BlockSpec(sel.shape, lambda i, j, be, nv, nb: (0, 0)),
            ],
            out_specs=pl.BlockSpec((tm, d), lambda i, j, be, nv, nb: (blk(i, nb), 0)),
            scratch_shapes=[pltpu.VMEM((tm, d), BF16)]),
        compiler_params=pltpu.CompilerParams(
            dimension_semantics=("arbitrary", "arbitrary"), vmem_limit_bytes=VMEM_LIMIT),
        name="experts",
    )(block_expert, block_valid, n_blocks_used, xs, w1, b1, w2, b2, sel)


def _combine_kernel(dest_ref, h_ref, gate_ref, nw_ref, y_ref, o_ref, buf_ref, sem, *, tc):
    i = pl.program_id(0)
    slot = i % 2

    def fetch(block, into):
        base = block * tc * TOP_K

        def issue(tile, carry):
            tile_rows = pl.ds(pl.multiple_of(tile * SUBLANES, SUBLANES), SUBLANES)
            for k in range(TOP_K):
                dst_tile = buf_ref.at[into, k, tile_rows]
                for u in range(SUBLANES):
                    pair = base + (tile * SUBLANES + u) * TOP_K + k
                    pltpu.make_async_copy(y_ref.at[pl.ds(dest_ref[pair], 1)], dst_tile.at[pl.ds(u, 1)],
                                          sem.at[into]).start(priority=k % 2)
            return carry

        lax.fori_loop(0, tc // SUBLANES, issue, 0)

    @pl.when(i == 0)
    def _():
        fetch(0, 0)

    @pl.when(i + 1 < pl.num_programs(0))
    def _():
        fetch(i + 1, 1 - slot)

    for k in range(TOP_K):
        pltpu.make_async_copy(y_ref.at[pl.ds(0, tc)], buf_ref.at[slot, k], sem.at[slot]).wait()
    gate = gate_ref[...]
    out = h_ref[...]
    for k in range(TOP_K):
        out = out + gate[:, k:k + 1] * buf_ref[slot, k]
    out = out * lax.rsqrt(jnp.mean(out * out, axis=-1, keepdims=True) + EPS) * nw_ref[...]
    o_ref[...] = out


def _combine(dest, h, gates, norm_w, y, tc=256):
    n, d = h.shape
    return pl.pallas_call(
        functools.partial(_combine_kernel, tc=tc),
        out_shape=jax.ShapeDtypeStruct((n, d), F32),
        grid_spec=pltpu.PrefetchScalarGridSpec(
            num_scalar_prefetch=1, grid=(n // tc,),
            in_specs=[
                pl.BlockSpec((tc, d), lambda i, dref: (i, 0)),
                pl.BlockSpec((tc, TOP_K), lambda i, dref: (i, 0)),
                pl.BlockSpec((1, d), lambda i, dref: (0, 0)),
                pl.BlockSpec(memory_space=pl.ANY),
            ],
            out_specs=pl.BlockSpec((tc, d), lambda i, dref: (i, 0)),
            scratch_shapes=[pltpu.VMEM((2, TOP_K, tc, d), F32), pltpu.SemaphoreType.DMA((2,))]),
        compiler_params=pltpu.CompilerParams(
            dimension_semantics=("arbitrary",), vmem_limit_bytes=VMEM_LIMIT),
        name="combine_norm",
    )(dest, h, gates, norm_w.reshape(1, d), y)


def _moe(h, hn, gates, idx, w1, b1, w2, b2, norm_final_w, tm=1024, tf=256):
    n, d = h.shape
    n_pairs = n * TOP_K
    e_flat = idx.reshape(n_pairs)
    rank, counts = _pair_ranks(idx, blk=min(512, n))
    counts = counts.reshape(N_EXPERTS)
    padded = (counts + tm - 1) // tm * tm
    end_pad = jnp.cumsum(padded)
    start_pad = end_pad - padded
    dest = start_pad[e_flat] + rank.reshape(n_pairs)
    dest = dest.astype(jnp.int32)
    m_pad = n_pairs + N_EXPERTS * tm
    n_blocks = m_pad // tm
    block_start = jnp.arange(n_blocks, dtype=jnp.int32) * tm
    block_expert = jnp.minimum(jnp.sum(end_pad[None, :] <= block_start[:, None], axis=1),
                               N_EXPERTS - 1).astype(jnp.int32)
    block_valid = jnp.clip((start_pad + counts)[block_expert] - block_start, 0, tm).astype(jnp.int32)
    n_blocks_used = (end_pad[-1:] // tm).astype(jnp.int32)
    xs = _dispatch_rows(hn, dest, m_pad, tc=min(512, n))
    y = _experts(xs, block_expert, block_valid, n_blocks_used, w1, b1[:, None, :], w2, b2[:, None, :], tm, tf)
    return _combine(dest, h, gates, norm_final_w, y)


def kernel(x, w_in, ret_norm_w, hg_norm_w, hg_lb_logits, w_out, norm_mix_w, norm_moe_w, w_router, b_router,
           w1, b1, w2, b2, norm_final_w):
    batch, seq, d = x.shape
    depth = w_in.shape[0]
    assert depth == 1, "the final norm is fused into the last layer's combine; one layer supported"
    lb_all = jnp.cumsum(jax.nn.softmax(hg_lb_logits.astype(F32), axis=0), axis=0)
    x2 = x.reshape(batch * seq, d)
    z = _inproj(x2, norm_mix_w[0], w_in[0].astype(BF16))
    o_r = _retention(z, ret_norm_w[0], batch, seq)
    o_h = _hgrn2(z, lb_all[0], hg_norm_w[0], batch, seq)
    h, hn, gates, idx = _outproj(o_r, o_h, x2, w_out[0].astype(BF16), norm_moe_w[0], w_router[0], b_router[0])
    out = _moe(h, hn, gates, idx, w1[0], b1[0], w2[0], b2[0], norm_final_w)
    return out.reshape(batch, seq, d)
```

```python
import functools

import numpy as np
import jax
import jax.numpy as jnp
from jax import lax
from jax.experimental import pallas as pl
from jax.experimental.pallas import tpu as pltpu

F32 = jnp.float32
BF16 = jnp.bfloat16

LANES = 128
SUBLANES = 8
CHUNK = 64
HG_LEVELS = (32, 16, 8, 4, 2, 1)
RET_HEADS = 8
HG_HEADS = 16
N_SPLITS = 10
N_EXPERTS = 32
TOP_K = 4
SWIGLU_ALPHA = 1.702
SWIGLU_LIMIT = 7.0
ROPE_BASE = 10000.0
EPS = 1e-6
LOG2_E = 1.4426950408889634
VMEM_LIMIT = 56 * 1024 * 1024

P_RQ, P_RK, P_RV, P_RG, P_HQ, P_HF, P_HI, P_HG, P_GR, P_GH = range(N_SPLITS)


def _sigmoid(x):
    return 1.0 / (1.0 + jnp.exp(-x))


def _sigmoid_pair(x):
    t = jnp.exp(-jnp.abs(x))
    r = 1.0 / (1.0 + t)
    p = t * r
    pos = x >= 0.0
    return jnp.where(pos, r, p), jnp.where(pos, p, r)


def _silu(x):
    return x * _sigmoid(x)


def _dot(a, b):
    return jnp.dot(a, b, preferred_element_type=F32)


def _dot_nt(a, b):
    return lax.dot_general(a, b, (((1,), (1,)), ((), ())), preferred_element_type=F32)


def _dot_tn(a, b):
    return lax.dot_general(a, b, (((0,), (0,)), ((), ())), preferred_element_type=F32)


def _inproj_kernel(x_ref, nw_ref, w_ref, o_ref, xn_ref):
    @pl.when(pl.program_id(1) == 0)
    def _():
        x = x_ref[...]
        ms = jnp.mean(x * x, axis=-1, keepdims=True)
        xn_ref[...] = (x * lax.rsqrt(ms + EPS) * nw_ref[...]).astype(BF16)

    acc = _dot(xn_ref[...], w_ref[...])
    for g in range(o_ref.shape[0]):
        o_ref[g] = acc[:, g * LANES:(g + 1) * LANES].astype(o_ref.dtype)


def _inproj(x2, norm_w, w_bf, tm=1024, tn=1024):
    n, d = x2.shape
    cols = w_bf.shape[1]
    gpt = tn // LANES
    return pl.pallas_call(
        _inproj_kernel,
        out_shape=jax.ShapeDtypeStruct((cols // LANES, n, LANES), BF16),
        grid=(n // tm, cols // tn),
        in_specs=[
            pl.BlockSpec((tm, d), lambda i, j: (i, 0)),
            pl.BlockSpec((1, d), lambda i, j: (0, 0)),
            pl.BlockSpec((d, tn), lambda i, j: (0, j)),
        ],
        out_specs=pl.BlockSpec((gpt, tm, LANES), lambda i, j: (j, i, 0)),
        scratch_shapes=[pltpu.VMEM((tm, d), BF16)],
        compiler_params=pltpu.CompilerParams(
            dimension_semantics=("parallel", "arbitrary"), vmem_limit_bytes=VMEM_LIMIT),
        name="inproj",
    )(x2, norm_w.reshape(1, d), w_bf)


def _retention_tables(seq, t_blk, dk):
    h = RET_HEADS
    log_gamma = jnp.log1p(-jnp.exp2(-5.0 - jnp.arange(h, dtype=F32)))
    inv = 1.0 / (ROPE_BASE ** jnp.linspace(0.0, 1.0, dk // 2, dtype=F32))
    ang = jnp.arange(seq, dtype=F32)[:, None] * inv[None, :]
    cos, sin = jnp.cos(ang), jnp.sin(ang)
    pos = np.arange(t_blk)
    cq, ck = pos[:, None] // CHUNK, pos[None, :] // CHUNK
    dist = (pos[:, None] - pos[None, :]).astype(np.float32)
    expo = np.where(cq == ck, np.abs(dist), dist)
    allowed = jnp.asarray(cq >= ck)
    dmask = jnp.where(allowed[None], jnp.exp(jnp.asarray(expo)[None] * log_gamma[:, None, None]), 0.0)
    posf = jnp.arange(t_blk, dtype=F32)
    qdec = jnp.exp((posf[None, :] + 1.0) * log_gamma[:, None])[..., None]
    kdec = jnp.exp((t_blk - 1.0 - posf)[None, :] * log_gamma[:, None])[..., None]
    sdec = jnp.broadcast_to(jnp.exp(t_blk * log_gamma)[:, None, None], (h, 1, LANES))
    return cos, sin, dmask, qdec, kdec, sdec


def _retention_kernel(q_ref, k_ref, v_ref, g_ref, gm_ref, cos_ref, sin_ref, dm_ref, qd_ref, kd_ref,
                      sd_ref, nw_ref, o_ref, st_ref, *, scale):
    @pl.when(pl.program_id(2) == 0)
    def _():
        st_ref[...] = jnp.zeros_like(st_ref)

    cos, sin = cos_ref[...], sin_ref[...]
    heads = st_ref.shape[0]
    gph = q_ref.shape[0] // heads
    dk = gph * LANES

    def slabs(ref, hd):
        return [ref[hd * gph + g] for g in range(gph)]

    def rot(ref, hd):
        x1, x2 = [s.astype(F32) for s in slabs(ref, hd)]
        return jnp.concatenate([x1 * cos - x2 * sin, x2 * cos + x1 * sin], axis=-1)

    def stage_rotate(hd):
        q = rot(q_ref, hd)
        k = rot(k_ref, hd) * scale
        return (q.astype(BF16), k.astype(BF16), (q * qd_ref[hd]).astype(BF16), (k * kd_ref[hd]).astype(BF16))

    def stage_matmuls(hd, rotated):
        q_bf, k_bf, q_dec, k_dec = rotated
        v = jnp.concatenate(slabs(v_ref, hd), axis=-1)
        scores = _dot_nt(q_bf, k_bf) * dm_ref[hd]
        state = st_ref[hd]
        o = _dot(scores.astype(BF16), v) + _dot(q_dec, state.astype(BF16))
        st_ref[hd] = sd_ref[hd][:, 0:1] * state + _dot_tn(k_dec, v)
        return o

    def stage_norm(hd, o):
        o = o - jnp.mean(o, axis=-1, keepdims=True)
        o = o * lax.rsqrt(jnp.mean(o * o, axis=-1, keepdims=True) + EPS)
        gate = jnp.concatenate(slabs(g_ref, hd), axis=-1).astype(F32)
        merge = jnp.concatenate(slabs(gm_ref, hd), axis=-1).astype(F32)
        res = _sigmoid(merge) * (o * nw_ref[:, hd * dk:(hd + 1) * dk] * _silu(gate))
        for g in range(gph):
            o_ref[hd * gph + g] = res[:, g * LANES:(g + 1) * LANES].astype(o_ref.dtype)

    rotated = [stage_rotate(hd) for hd in range(heads)]
    outs = [stage_matmuls(hd, rotated[hd]) for hd in range(heads)]
    for hd in range(heads):
        stage_norm(hd, outs[hd])


def _retention(z, ret_norm_w, batch, seq, t_blk=512, heads=4):
    n = batch * seq
    d = ret_norm_w.shape[0]
    dk = d // RET_HEADS
    gps = heads * dk // LANES
    nt = seq // t_blk
    cos, sin, dmask, qdec, kdec, sdec = _retention_tables(seq, t_blk, dk)

    def part(p):
        return pl.BlockSpec((gps, t_blk, LANES), lambda b, h, t, p=p: (p * RET_HEADS // heads + h, b * nt + t, 0))

    per_head = lambda shape: pl.BlockSpec((heads,) + shape, lambda b, h, t: (h, 0, 0))
    return pl.pallas_call(
        functools.partial(_retention_kernel, scale=dk ** -0.5),
        out_shape=jax.ShapeDtypeStruct((d // LANES, n, LANES), BF16),
        grid=(batch, RET_HEADS // heads, nt),
        in_specs=[
            part(P_RQ), part(P_RK), part(P_RV), part(P_RG), part(P_GR),
            pl.BlockSpec((t_blk, dk // 2), lambda b, h, t: (t, 0)),
            pl.BlockSpec((t_blk, dk // 2), lambda b, h, t: (t, 0)),
            per_head((t_blk, t_blk)), per_head((t_blk, 1)), per_head((t_blk, 1)), per_head((1, LANES)),
            pl.BlockSpec((1, heads * dk), lambda b, h, t: (0, h)),
        ],
        out_specs=pl.BlockSpec((gps, t_blk, LANES), lambda b, h, t: (h, b * nt + t, 0)),
        scratch_shapes=[pltpu.VMEM((heads, dk, dk), F32)],
        compiler_params=pltpu.CompilerParams(
            dimension_semantics=("parallel", "parallel", "arbitrary"), vmem_limit_bytes=VMEM_LIMIT),
        name="retention",
    )(z, z, z, z, z, cos, sin, dmask, qdec, kdec, sdec, ret_norm_w.reshape(1, d))


def _hgrn_level_tables():
    c = CHUNK
    r = np.arange(c)
    mats = [(r[None, :] <= r[:, None])]
    masks = [np.eye(c, dtype=bool)]
    for s in HG_LEVELS:
        blk, within = r // (2 * s), r % (2 * s)
        mid = blk * 2 * s + s - 1
        right = within >= s
        t = r[None, :]
        if s < SUBLANES:
            mats.append(np.where(right[:, None], (t > mid[:, None]) & (t <= r[:, None]),
                                 (t > r[:, None]) & (t <= mid[:, None])))
        masks.append((blk[:, None] == blk[None, :]) & right[:, None] & (~right)[None, :])
    wmat = np.concatenate(mats, axis=0).astype(np.float32)
    if len(masks) % 2:
        masks.append(np.zeros((c, c), bool))
    mask = np.stack([np.concatenate(masks[p:p + 2], axis=1) for p in range(0, len(masks), 2)])
    return jnp.asarray(wmat, BF16), jnp.asarray(mask.astype(np.float32), F32)


def _split2(x):
    hi = x.astype(BF16)
    return hi, (x - hi.astype(F32)).astype(BF16)


def _level_exponent(cum, s):
    pieces = []
    for r0 in range(0, CHUNK, SUBLANES):
        mid = r0 // (2 * s) * 2 * s + s - 1
        rows, ref = cum[r0:r0 + SUBLANES], cum[mid:mid + 1]
        pieces.append(rows - ref if r0 % (2 * s) >= s else ref - rows)
    return jnp.concatenate(pieces, axis=0)


def _hgrn_kernel(q_ref, f_ref, i_ref, g_ref, gm_ref, lb_ref, nw_ref, w_ref, m_ref, o_ref,
                 st_ref, qs_ref, ks_ref, ob_ref, qe_ref, kd_ref, dec_ref, el_ref, s_ref, kv_ref):
    @pl.when(pl.program_id(2) == 0)
    def _():
        st_ref[...] = jnp.zeros_like(st_ref)

    c = CHUNK
    t_blk, dk = qs_ref.shape
    n_lvl = len(HG_LEVELS)
    n_chunks = t_blk // c
    lb = lb_ref[...]
    zeros = jnp.zeros((c, dk), BF16)
    state = [st_ref[...]]

    def stage_gates(ci):
        rows = slice(ci * c, (ci + 1) * c)
        z = f_ref[rows, :].astype(F32)
        qs = _silu(q_ref[rows, :].astype(F32))
        sig_pos, sig_neg = _sigmoid_pair(z)
        ks = (1.0 - lb) * sig_neg
        qs_ref[rows, :] = qs.astype(BF16)
        ks_ref[rows, :] = ks.astype(BF16)
        hi, lo = _split2(jnp.log(lb + (1.0 - lb) * sig_pos) * LOG2_E)
        g2 = _dot(w_ref[...], jnp.concatenate([hi, lo], axis=-1))
        g = g2[:, :dk] + g2[:, dk:]
        cum = g[0:c]
        e_start = jnp.exp2(cum)
        qe_ref[rows, :] = (qs * e_start).astype(BF16)
        kd_ref[rows, :] = (ks * jnp.exp2(cum[c - 1:c] - cum)).astype(BF16)
        dec_ref[ci] = e_start[c - 1:c]
        narrow = 1
        for l, s in enumerate(HG_LEVELS):
            if s >= SUBLANES:
                expo = _level_exponent(cum, s)
            else:
                expo = g[narrow * c:(narrow + 1) * c]
                narrow += 1
            el_ref[(ci * n_lvl + l) * c:(ci * n_lvl + l + 1) * c, :] = jnp.exp2(expo).astype(BF16)

    def stage_scores(ci):
        rows = slice(ci * c, (ci + 1) * c)
        qb, kb = qs_ref[rows, :], ks_ref[rows, :]
        lhs, rhs = [qb], [kb]
        for l in range(n_lvl):
            el = el_ref[(ci * n_lvl + l) * c:(ci * n_lvl + l + 1) * c, :]
            lhs.append(qb * el)
            rhs.append(kb * el)
        s = None
        for p in range(m_ref.shape[0]):
            a0, b0 = lhs[2 * p], rhs[2 * p]
            if 2 * p + 1 < len(lhs):
                a = jnp.concatenate([a0, lhs[2 * p + 1]], axis=1)
                b = jnp.concatenate([jnp.concatenate([b0, zeros], axis=1),
                                     jnp.concatenate([zeros, rhs[2 * p + 1]], axis=1)], axis=0)
            else:
                a, b = a0, jnp.concatenate([b0, zeros], axis=0)
            tile = m_ref[p] * _dot_nt(a, b)
            s = tile if s is None else s + tile
        s_ref[rows, :] = s.astype(BF16)

    def stage_intra(ci):
        rows = slice(ci * c, (ci + 1) * c)
        vc = i_ref[rows, :]
        ob_ref[rows, :] = _dot(s_ref[rows, :], jnp.concatenate([vc, vc], axis=0))
        kv_ref[ci] = _dot_tn(vc, kd_ref[rows, :])

    def stage_state(ci):
        rows = slice(ci * c, (ci + 1) * c)
        st = state[0]
        o = ob_ref[rows, :] + _dot_nt(qe_ref[rows, :], st.astype(BF16))
        state[0] = st * dec_ref[ci] + kv_ref[ci]
        o = o * lax.rsqrt(jnp.mean(o * o, axis=-1, keepdims=True) + EPS)
        o = o * nw_ref[...] * _silu(g_ref[rows, :].astype(F32))
        o_ref[rows, :] = (_sigmoid(gm_ref[rows, :].astype(F32)) * o).astype(o_ref.dtype)

    stages = (stage_gates, stage_scores, stage_intra, stage_state)
    for step in range(n_chunks + len(stages) - 1):
        for depth, stage in enumerate(stages):
            if 0 <= step - depth < n_chunks:
                stage(step - depth)
    st_ref[...] = state[0]


def _hgrn2(z, lb, hg_norm_w, batch, seq, t_blk=4096):
    n = batch * seq
    d = hg_norm_w.shape[0]
    dk = d // HG_HEADS
    t_blk = min(t_blk, seq)
    nt = seq // t_blk
    wmat, mask = _hgrn_level_tables()

    def part(p):
        return pl.BlockSpec((None, t_blk, LANES), lambda b, h, t, p=p: (p * HG_HEADS + h, b * nt + t, 0))

    head_vec = pl.BlockSpec((1, dk), lambda b, h, t: (0, h))
    return pl.pallas_call(
        _hgrn_kernel,
        out_shape=jax.ShapeDtypeStruct((d // LANES, n, LANES), BF16),
        grid=(batch, HG_HEADS, nt),
        in_specs=[
            part(P_HQ), part(P_HF), part(P_HI), part(P_HG), part(P_GH), head_vec, head_vec,
            pl.BlockSpec(wmat.shape, lambda b, h, t: (0, 0)),
            pl.BlockSpec(mask.shape, lambda b, h, t: (0, 0, 0)),
        ],
        out_specs=pl.BlockSpec((None, t_blk, LANES), lambda b, h, t: (h, b * nt + t, 0)),
        scratch_shapes=[
            pltpu.VMEM((dk, dk), F32), pltpu.VMEM((t_blk, dk), BF16), pltpu.VMEM((t_blk, dk), BF16),
            pltpu.VMEM((t_blk, dk), F32),
            pltpu.VMEM((t_blk, dk), BF16), pltpu.VMEM((t_blk, dk), BF16),
            pltpu.VMEM((t_blk // CHUNK, 1, dk), F32),
            pltpu.VMEM((t_blk * len(HG_LEVELS), dk), BF16), pltpu.VMEM((t_blk, 2 * CHUNK), BF16),
            pltpu.VMEM((t_blk // CHUNK, dk, dk), F32),
        ],
        compiler_params=pltpu.CompilerParams(
            dimension_semantics=("parallel", "parallel", "arbitrary"), vmem_limit_bytes=VMEM_LIMIT),
        name="hgrn2",
    )(z, z, z, z, z, lb.reshape(1, d), hg_norm_w.reshape(1, d), wmat, mask)


def _outproj_kernel(a_ref, b_ref, x_ref, w_ref, nw_ref, wr_ref, br_ref, h_ref, hn_ref, gate_ref, idx_ref):
    merged = jnp.concatenate([(a_ref[g].astype(F32) + b_ref[g].astype(F32)).astype(BF16)
                              for g in range(a_ref.shape[0])], axis=1)
    h = x_ref[...] + _dot(merged, w_ref[...])
    h_ref[...] = h
    hn = h * lax.rsqrt(jnp.mean(h * h, axis=-1, keepdims=True) + EPS) * nw_ref[...]
    hn_ref[...] = hn
    n_e = br_ref.shape[-1]
    hn_hi, hn_lo = _split2(hn)
    l2 = _dot(hn_hi, wr_ref[...])
    logits = l2[:, :n_e] + l2[:, n_e:] + _dot(hn_lo, wr_ref[:, :n_e]) + br_ref[...]
    lane = lax.broadcasted_iota(jnp.int32, logits.shape, 1).astype(F32)
    vals, idxs = [], []
    for _ in range(TOP_K):
        m = jnp.max(logits, axis=-1, keepdims=True)
        idx = jnp.min(jnp.where(logits == m, lane, float(n_e)), axis=-1, keepdims=True)
        vals.append(m)
        idxs.append(idx)
        logits = jnp.where(lane == idx, -jnp.inf, logits)
    exps = [jnp.exp(v - vals[0]) for v in vals]
    denom = exps[0] + exps[1] + exps[2] + exps[3]
    for k in range(TOP_K):
        gate_ref[:, k:k + 1] = exps[k] / denom
        idx_ref[:, k:k + 1] = idxs[k].astype(jnp.int32)


def _outproj(o_r, o_h, x2, w_bf, norm_w, w_router, b_router, tm=512):
    n, d = x2.shape
    n_e = w_router.shape[1]
    row = lambda w: pl.BlockSpec((tm, w), lambda i: (i, 0))
    full = lambda a: pl.BlockSpec(a.shape, lambda i: (0, 0))
    slabs = pl.BlockSpec((d // LANES, tm, LANES), lambda i: (0, i, 0))
    nw, br = norm_w.reshape(1, d), b_router.reshape(1, n_e)
    wr = jnp.concatenate(_split2(w_router), axis=1)
    return pl.pallas_call(
        _outproj_kernel,
        out_shape=(jax.ShapeDtypeStruct((n, d), F32), jax.ShapeDtypeStruct((n, d), F32),
                   jax.ShapeDtypeStruct((n, TOP_K), F32), jax.ShapeDtypeStruct((n, TOP_K), jnp.int32)),
        grid=(n // tm,),
        in_specs=[slabs, slabs, row(d), full(w_bf), full(nw), full(wr), full(br)],
        out_specs=(row(d), row(d), row(TOP_K), row(TOP_K)),
        compiler_params=pltpu.CompilerParams(
            dimension_semantics=("parallel",), vmem_limit_bytes=VMEM_LIMIT),
        name="outproj_router",
    )(o_r, o_h, x2, w_bf, nw, wr, br)


def _rank_kernel(idx_ref, tri_ref, rank_ref, cnt_ref, run_ref):
    @pl.when(pl.program_id(0) == 0)
    def _():
        run_ref[...] = jnp.zeros_like(run_ref)

    idx = idx_ref[...]
    lane = lax.broadcasted_iota(jnp.int32, (idx.shape[0], N_EXPERTS), 1)
    run = run_ref[...]
    for k in range(TOP_K):
        onehot = lane == idx[:, k:k + 1]
        before = _dot(tri_ref[...], onehot.astype(BF16)) + run
        rank_ref[:, k:k + 1] = jnp.sum(jnp.where(onehot, before, 0.0), axis=-1, keepdims=True).astype(jnp.int32)
        run = run + jnp.sum(onehot.astype(F32), axis=0, keepdims=True)
    run_ref[...] = run
    cnt_ref[...] = run.astype(jnp.int32)


def _pair_ranks(idx, blk=512):
    n = idx.shape[0]
    r = np.arange(blk)
    tri = jnp.asarray(r[None, :] < r[:, None], BF16)
    return pl.pallas_call(
        _rank_kernel,
        out_shape=(jax.ShapeDtypeStruct((n, TOP_K), jnp.int32), jax.ShapeDtypeStruct((1, N_EXPERTS), jnp.int32)),
        grid=(n // blk,),
        in_specs=[pl.BlockSpec((blk, TOP_K), lambda i: (i, 0)), pl.BlockSpec((blk, blk), lambda i: (0, 0))],
        out_specs=(pl.BlockSpec((blk, TOP_K), lambda i: (i, 0)), pl.BlockSpec((1, N_EXPERTS), lambda i: (0, 0))),
        scratch_shapes=[pltpu.VMEM((1, N_EXPERTS), F32)],
        compiler_params=pltpu.CompilerParams(dimension_semantics=("arbitrary",)),
        name="pair_ranks",
    )(idx, tri)


def _dispatch_kernel(dest_ref, src_ref, dst_ref, sem, *, tc):
    base = pl.program_id(0) * tc * TOP_K

    def issue(tile, carry):
        src_tile = src_ref.at[pl.ds(pl.multiple_of(tile * SUBLANES, SUBLANES), SUBLANES)]
        for u in range(SUBLANES):
            for k in range(TOP_K):
                pair = base + (tile * SUBLANES + u) * TOP_K + k
                pltpu.make_async_copy(src_tile.at[pl.ds(u, 1)], dst_ref.at[pl.ds(dest_ref[pair], 1)],
                                      sem).start(priority=k % 2)
        return carry

    lax.fori_loop(0, tc // SUBLANES, issue, 0)
    for _ in range(TOP_K):
        pltpu.make_async_copy(src_ref, dst_ref.at[pl.ds(0, tc)], sem).wait()


def _dispatch_rows(src, dest, m_pad, tc=256):
    n, d = src.shape
    return pl.pallas_call(
        functools.partial(_dispatch_kernel, tc=tc),
        out_shape=jax.ShapeDtypeStruct((m_pad, d), src.dtype),
        grid_spec=pltpu.PrefetchScalarGridSpec(
            num_scalar_prefetch=1, grid=(n // tc,),
            in_specs=[pl.BlockSpec((tc, d), lambda i, dref: (i, 0))],
            out_specs=pl.BlockSpec(memory_space=pl.ANY),
            scratch_shapes=[pltpu.SemaphoreType.DMA]),
        compiler_params=pltpu.CompilerParams(
            dimension_semantics=("arbitrary",), vmem_limit_bytes=VMEM_LIMIT),
        name="dispatch_rows",
    )(dest, src)


PAIR = 2 * LANES
SUB_ROWS = 128
SPLIT_ROWS = 512


def _even_lane_selector():
    c = np.arange(LANES)
    blk = np.zeros((LANES, LANES), np.float32)
    blk[c, c // 2 + LANES // 2 * (c % 2)] = 1.0
    return jnp.asarray(np.kron(np.eye(2, dtype=np.float32), blk), BF16)


def _expert_kernel(be_ref, nv_ref, nb_ref, x_ref, w1_ref, b1_ref, w2_ref, b2_ref, sel_ref, y_ref,
                   xb_ref):
    i, j = pl.program_id(0), pl.program_id(1)
    n_valid = nv_ref[i]
    tm = xb_ref.shape[0]

    @pl.when(i < nb_ref[0])
    def _():
        @pl.when(j == 0)
        def _():
            row = lax.broadcasted_iota(jnp.int32, (tm, 1), 0)
            xb_ref[...] = jnp.where(row < n_valid, x_ref[...], 0.0).astype(BF16)

        def rows_pass(start_row, n_rows):
            align = min(SUB_ROWS, tm)
            halves = 2 if n_rows >= SPLIT_ROWS else 1
            hr = n_rows // halves
            even = lax.broadcasted_iota(jnp.int32, (hr, LANES), 1) % 2 == 0
            w1b, w2b = w1_ref[...].astype(BF16), w2_ref[...].astype(BF16)
            hcats = []
            for p in range(halves):
                row0 = pl.multiple_of(start_row + p * hr, align)
                hcats.append((row0, _dot(xb_ref[pl.ds(row0, hr), :], w1b) + b1_ref[...]))
            for row0, hcat in hcats:
                acts = []
                for g in range(hcat.shape[1] // PAIR):
                    h0, h1 = hcat[:, g * PAIR:g * PAIR + LANES], hcat[:, g * PAIR + LANES:(g + 1) * PAIR]
                    glu = jnp.where(even, h0, pltpu.roll(h1, 1, 1))
                    lin = jnp.where(even, pltpu.roll(h0, LANES - 1, 1), h1)
                    glu = jnp.minimum(glu, SWIGLU_LIMIT)
                    lin = jnp.clip(lin, -SWIGLU_LIMIT, SWIGLU_LIMIT)
                    acts.append((glu * _sigmoid(SWIGLU_ALPHA * glu) * (lin + 1.0)).astype(BF16))
                hid = []
                for g in range(0, len(acts), 2):
                    width = LANES * len(acts[g:g + 2])
                    hid.append(_dot(jnp.concatenate(acts[g:g + 2], axis=1),
                                    sel_ref[0:width, 0:width]).astype(BF16))
                part = _dot(jnp.concatenate(hid, axis=1), w2b)
                rows = pl.ds(row0, hr)
                start = jnp.where(j == 0, jnp.broadcast_to(b2_ref[...], part.shape), y_ref[rows, :])
                y_ref[rows, :] = start + part

        unit = min(SUB_ROWS, tm)
        n_units = (n_valid + unit - 1) // unit
        for bit in reversed(range((tm // unit).bit_length())):
            @pl.when(((n_units >> bit) & 1) == 1)
            def _(bit=bit):
                rows_pass((n_units >> (bit + 1) << (bit + 1)) * unit, unit << bit)


def _experts(xs, block_expert, block_valid, n_blocks_used, w1, b1, w2, b2, tm, tf=256):
    m_pad, d = xs.shape
    f = w2.shape[1]
    tf = min(tf, f)
    nf = f // tf
    nblk = m_pad // tm
    sel = _even_lane_selector()

    def blk(i, nb):
        return jnp.minimum(i, nb[0] - 1)

    def ftile(i, j, nb):
        return jnp.where(i < nb[0], j, nf - 1)

    return pl.pallas_call(
        _expert_kernel,
        out_shape=jax.ShapeDtypeStruct((m_pad, d), F32),
        grid_spec=pltpu.PrefetchScalarGridSpec(
            num_scalar_prefetch=3, grid=(nblk, nf),
            in_specs=[
                pl.BlockSpec((tm, d), lambda i, j, be, nv, nb: (blk(i, nb), 0)),
                pl.BlockSpec((None, d, 2 * tf), lambda i, j, be, nv, nb: (be[blk(i, nb)], 0, ftile(i, j, nb))),
                pl.BlockSpec((None, 1, 2 * tf), lambda i, j, be, nv, nb: (be[blk(i, nb)], 0, ftile(i, j, nb))),
                pl.BlockSpec((None, tf, d), lambda i, j, be, nv, nb: (be[blk(i, nb)], ftile(i, j, nb), 0)),
                pl.BlockSpec((None, 1, d), lambda i, j, be, nv, nb: (be[blk(i, nb)], 0, 0)),
                pl.BlockSpec(sel.shape, lambda i, j, be, nv, nb: (0, 0)),
            ],
            out_specs=pl.BlockSpec((tm, d), lambda i, j, be, nv, nb: (blk(i, nb), 0)),
            scratch_shapes=[pltpu.VMEM((tm, d), BF16)]),
        compiler_params=pltpu.CompilerParams(
            dimension_semantics=("arbitrary", "arbitrary"), vmem_limit_bytes=VMEM_LIMIT),
        name="experts",
    )(block_expert, block_valid, n_blocks_used, xs, w1, b1, w2, b2, sel)


def _combine_kernel(dest_ref, h_ref, gate_ref, nw_ref, y_ref, o_ref, buf_ref, sem, *, tc):
    i = pl.program_id(0)
    slot = i % 2

    def fetch(block, into):
        base = block * tc * TOP_K

        def issue(tile, carry):
            tile_rows = pl.ds(pl.multiple_of(tile * SUBLANES, SUBLANES), SUBLANES)
            for k in range(TOP_K):
                dst_tile = buf_ref.at[into, k, tile_rows]
                for u in range(SUBLANES):
                    pair = base + (tile * SUBLANES + u) * TOP_K + k
                    pltpu.make_async_copy(y_ref.at[pl.ds(dest_ref[pair], 1)], dst_tile.at[pl.ds(u, 1)],
                                          sem.at[into]).start(priority=k % 2)
            return carry

        lax.fori_loop(0, tc // SUBLANES, issue, 0)

    @pl.when(i == 0)
    def _():
        fetch(0, 0)

    @pl.when(i + 1 < pl.num_programs(0))
    def _():
        fetch(i + 1, 1 - slot)

    for k in range(TOP_K):
        pltpu.make_async_copy(y_ref.at[pl.ds(0, tc)], buf_ref.at[slot, k], sem.at[slot]).wait()
    gate = gate_ref[...]
    out = h_ref[...]
    for k in range(TOP_K):
        out = out + gate[:, k:k + 1] * buf_ref[slot, k]
    out = out * lax.rsqrt(jnp.mean(out * out, axis=-1, keepdims=True) + EPS) * nw_ref[...]
    o_ref[...] = out


def _combine(dest, h, gates, norm_w, y, tc=256):
    n, d = h.shape
    return pl.pallas_call(
        functools.partial(_combine_kernel, tc=tc),
        out_shape=jax.ShapeDtypeStruct((n, d), F32),
        grid_spec=pltpu.PrefetchScalarGridSpec(
            num_scalar_prefetch=1, grid=(n // tc,),
            in_specs=[
                pl.BlockSpec((tc, d), lambda i, dref: (i, 0)),
                pl.BlockSpec((tc, TOP_K), lambda i, dref: (i, 0)),
                pl.BlockSpec((1, d), lambda i, dref: (0, 0)),
                pl.BlockSpec(memory_space=pl.ANY),
            ],
            out_specs=pl.BlockSpec((tc, d), lambda i, dref: (i, 0)),
            scratch_shapes=[pltpu.VMEM((2, TOP_K, tc, d), F32), pltpu.SemaphoreType.DMA((2,))]),
        compiler_params=pltpu.CompilerParams(
            dimension_semantics=("arbitrary",), vmem_limit_bytes=VMEM_LIMIT),
        name="combine_norm",
    )(dest, h, gates, norm_w.reshape(1, d), y)


def _moe(h, hn, gates, idx, w1, b1, w2, b2, norm_final_w, tm=1024, tf=256):
    n, d = h.shape
    n_pairs = n * TOP_K
    e_flat = idx.reshape(n_pairs)
    rank, counts = _pair_ranks(idx, blk=min(512, n))
    counts = counts.reshape(N_EXPERTS)
    padded = (counts + tm - 1) // tm * tm
    end_pad = jnp.cumsum(padded)
    start_pad = end_pad - padded
    dest = start_pad[e_flat] + rank.reshape(n_pairs)
    dest = dest.astype(jnp.int32)
    m_pad = -(-(n_pairs + N_EXPERTS * tm) // tm) * tm
    n_blocks = m_pad // tm
    block_start = jnp.arange(n_blocks, dtype=jnp.int32) * tm
    block_expert = jnp.minimum(jnp.sum(end_pad[None, :] <= block_start[:, None], axis=1),
                               N_EXPERTS - 1).astype(jnp.int32)
    block_valid = jnp.clip((start_pad + counts)[block_expert] - block_start, 0, tm).astype(jnp.int32)
    n_blocks_used = (end_pad[-1:] // tm).astype(jnp.int32)
    xs = _dispatch_rows(hn, dest, m_pad, tc=min(512, n))
    y = _experts(xs, block_expert, block_valid, n_blocks_used, w1, b1[:, None, :], w2, b2[:, None, :], tm, tf)
    return _combine(dest, h, gates, norm_final_w, y)


def kernel(x, w_in, ret_norm_w, hg_norm_w, hg_lb_logits, w_out, norm_mix_w, norm_moe_w, w_router, b_router,
           w1, b1, w2, b2, norm_final_w):
    batch, seq, d = x.shape
    depth = w_in.shape[0]
    assert depth == 1, "the final norm is fused into the last layer's combine; one layer supported"
    lb_all = jnp.cumsum(jax.nn.softmax(hg_lb_logits.astype(F32), axis=0), axis=0)
    x2 = x.reshape(batch * seq, d)
    z = _inproj(x2, norm_mix_w[0], w_in[0].astype(BF16))
    o_r = _retention(z, ret_norm_w[0], batch, seq)
    o_h = _hgrn2(z, lb_all[0], hg_norm_w[0], batch, seq)
    h, hn, gates, idx = _outproj(o_r, o_h, x2, w_out[0].astype(BF16), norm_moe_w[0], w_router[0], b_router[0])
    out = _moe(h, hn, gates, idx, w1[0], b1[0], w2[0], b2[0], norm_final_w)
    return out.reshape(batch, seq, d)
```

```python
import functools

import numpy as np
import jax
import jax.numpy as jnp
from jax import lax
from jax.experimental import pallas as pl
from jax.experimental.pallas import tpu as pltpu

F32 = jnp.float32
BF16 = jnp.bfloat16

LANES = 128
SUBLANES = 8
CHUNK = 64
HG_LEVELS = (32, 16, 8, 4, 2, 1)
RET_HEADS = 8
HG_HEADS = 16
N_SPLITS = 10
N_EXPERTS = 32
TOP_K = 4
SWIGLU_ALPHA = 1.702
SWIGLU_LIMIT = 7.0
ROPE_BASE = 10000.0
EPS = 1e-6
LOG2_E = 1.4426950408889634
VMEM_LIMIT = 56 * 1024 * 1024

P_RQ, P_RK, P_RV, P_RG, P_HQ, P_HF, P_HI, P_HG, P_GR, P_GH = range(N_SPLITS)


def _sigmoid(x, scale=1.0):
    return 1.0 / (1.0 + jnp.exp2(x * (-scale * LOG2_E)))


def _sigmoid_pair(x):
    t = jnp.exp(-jnp.abs(x))
    r = 1.0 / (1.0 + t)
    p = t * r
    pos = x >= 0.0
    return jnp.where(pos, r, p), jnp.where(pos, p, r)


def _silu(x):
    return x * _sigmoid(x)


def _dot(a, b):
    return jnp.dot(a, b, preferred_element_type=F32)


def _dot_nt(a, b):
    return lax.dot_general(a, b, (((1,), (1,)), ((), ())), preferred_element_type=F32)


def _dot_tn(a, b):
    return lax.dot_general(a, b, (((0,), (0,)), ((), ())), preferred_element_type=F32)


def _inproj_kernel(x_ref, nw_ref, w_ref, o_ref, xn_ref):
    @pl.when(pl.program_id(1) == 0)
    def _():
        x = x_ref[...]
        ms = jnp.mean(x * x, axis=-1, keepdims=True)
        xn_ref[...] = (x * lax.rsqrt(ms + EPS) * nw_ref[...]).astype(BF16)

    acc = _dot(xn_ref[...], w_ref[...].astype(BF16))
    for g in range(o_ref.shape[0]):
        o_ref[g] = acc[:, g * LANES:(g + 1) * LANES].astype(o_ref.dtype)


def _inproj(x2, norm_w, w_bf, tm=1024, tn=1024):
    n, d = x2.shape
    cols = w_bf.shape[1]
    gpt = tn // LANES
    return pl.pallas_call(
        _inproj_kernel,
        out_shape=jax.ShapeDtypeStruct((cols // LANES, n, LANES), BF16),
        grid=(n // tm, cols // tn),
        in_specs=[
            pl.BlockSpec((tm, d), lambda i, j: (i, 0)),
            pl.BlockSpec((1, d), lambda i, j: (0, 0)),
            pl.BlockSpec((d, tn), lambda i, j: (0, j)),
        ],
        out_specs=pl.BlockSpec((gpt, tm, LANES), lambda i, j: (j, i, 0)),
        scratch_shapes=[pltpu.VMEM((tm, d), BF16)],
        compiler_params=pltpu.CompilerParams(
            dimension_semantics=("parallel", "arbitrary"), vmem_limit_bytes=VMEM_LIMIT),
        name="inproj",
    )(x2, norm_w.reshape(1, d), w_bf)


def _retention_tables(seq, t_blk, dk):
    h = RET_HEADS
    log_gamma = jnp.log1p(-jnp.exp2(-5.0 - jnp.arange(h, dtype=F32)))
    inv = 1.0 / (ROPE_BASE ** jnp.linspace(0.0, 1.0, dk // 2, dtype=F32))
    ang = jnp.arange(seq, dtype=F32)[:, None] * inv[None, :]
    cos, sin = jnp.cos(ang), jnp.sin(ang)
    pos = np.arange(t_blk)
    cq, ck = pos[:, None] // CHUNK, pos[None, :] // CHUNK
    dist = (pos[:, None] - pos[None, :]).astype(np.float32)
    expo = np.where(cq == ck, np.abs(dist), dist)
    allowed = jnp.asarray(cq >= ck)
    dmask = jnp.where(allowed[None], jnp.exp(jnp.asarray(expo)[None] * log_gamma[:, None, None]), 0.0)
    posf = jnp.arange(t_blk, dtype=F32)
    qdec = jnp.exp((posf[None, :] + 1.0) * log_gamma[:, None])[..., None]
    kdec = jnp.exp((t_blk - 1.0 - posf)[None, :] * log_gamma[:, None])[..., None]
    sdec = jnp.broadcast_to(jnp.exp(t_blk * log_gamma)[:, None, None], (h, 1, LANES))
    return cos, sin, dmask, qdec, kdec, sdec


def _retention_kernel(q_ref, k_ref, v_ref, g_ref, gm_ref, cos_ref, sin_ref, dm_ref, qd_ref, kd_ref,
                      sd_ref, nw_ref, o_ref, st_ref, *, scale):
    @pl.when(pl.program_id(2) == 0)
    def _():
        st_ref[...] = jnp.zeros_like(st_ref)

    cos, sin = cos_ref[...], sin_ref[...]
    heads = st_ref.shape[0]
    gph = q_ref.shape[0] // heads
    dk = gph * LANES

    def slabs(ref, hd):
        return [ref[hd * gph + g] for g in range(gph)]

    def rot(ref, hd):
        x1, x2 = [s.astype(F32) for s in slabs(ref, hd)]
        return jnp.concatenate([x1 * cos - x2 * sin, x2 * cos + x1 * sin], axis=-1)

    def stage_rotate(hd):
        q = rot(q_ref, hd)
        k = rot(k_ref, hd) * scale
        return (q.astype(BF16), k.astype(BF16), (q * qd_ref[hd]).astype(BF16), (k * kd_ref[hd]).astype(BF16))

    def stage_matmuls(hd, rotated):
        q_bf, k_bf, q_dec, k_dec = rotated
        v = jnp.concatenate(slabs(v_ref, hd), axis=-1)
        scores = _dot_nt(q_bf, k_bf) * dm_ref[hd]
        state = st_ref[hd]
        o = _dot(scores.astype(BF16), v) + _dot(q_dec, state.astype(BF16))
        st_ref[hd] = sd_ref[hd][:, 0:1] * state + _dot_tn(k_dec, v)
        return o

    def stage_norm(hd, o):
        o = o - jnp.mean(o, axis=-1, keepdims=True)
        o = o * lax.rsqrt(jnp.mean(o * o, axis=-1, keepdims=True) + EPS)
        gate = jnp.concatenate(slabs(g_ref, hd), axis=-1).astype(F32)
        merge = jnp.concatenate(slabs(gm_ref, hd), axis=-1).astype(F32)
        res = _sigmoid(merge) * (o * nw_ref[:, hd * dk:(hd + 1) * dk] * _silu(gate))
        for g in range(gph):
            o_ref[hd * gph + g] = res[:, g * LANES:(g + 1) * LANES].astype(o_ref.dtype)

    rotated = [stage_rotate(hd) for hd in range(heads)]
    outs = [stage_matmuls(hd, rotated[hd]) for hd in range(heads)]
    for hd in range(heads):
        stage_norm(hd, outs[hd])


def _retention(z, ret_norm_w, batch, seq, t_blk=512, heads=4):
    n = batch * seq
    d = ret_norm_w.shape[0]
    dk = d // RET_HEADS
    gps = heads * dk // LANES
    nt = seq // t_blk
    cos, sin, dmask, qdec, kdec, sdec = _retention_tables(seq, t_blk, dk)

    def part(p):
        return pl.BlockSpec((gps, t_blk, LANES), lambda b, h, t, p=p: (p * RET_HEADS // heads + h, b * nt + t, 0))

    per_head = lambda shape: pl.BlockSpec((heads,) + shape, lambda b, h, t: (h, 0, 0))
    return pl.pallas_call(
        functools.partial(_retention_kernel, scale=dk ** -0.5),
        out_shape=jax.ShapeDtypeStruct((d // LANES, n, LANES), BF16),
        grid=(batch, RET_HEADS // heads, nt),
        in_specs=[
            part(P_RQ), part(P_RK), part(P_RV), part(P_RG), part(P_GR),
            pl.BlockSpec((t_blk, dk // 2), lambda b, h, t: (t, 0)),
            pl.BlockSpec((t_blk, dk // 2), lambda b, h, t: (t, 0)),
            per_head((t_blk, t_blk)), per_head((t_blk, 1)), per_head((t_blk, 1)), per_head((1, LANES)),
            pl.BlockSpec((1, heads * dk), lambda b, h, t: (0, h)),
        ],
        out_specs=pl.BlockSpec((gps, t_blk, LANES), lambda b, h, t: (h, b * nt + t, 0)),
        scratch_shapes=[pltpu.VMEM((heads, dk, dk), F32)],
        compiler_params=pltpu.CompilerParams(
            dimension_semantics=("parallel", "parallel", "arbitrary"), vmem_limit_bytes=VMEM_LIMIT),
        name="retention",
    )(z, z, z, z, z, cos, sin, dmask, qdec, kdec, sdec, ret_norm_w.reshape(1, d))


def _hgrn_level_tables():
    c = CHUNK
    r = np.arange(c)
    mats = [(r[None, :] <= r[:, None])]
    masks = [np.eye(c, dtype=bool)]
    for s in HG_LEVELS:
        blk, within = r // (2 * s), r % (2 * s)
        mid = blk * 2 * s + s - 1
        right = within >= s
        t = r[None, :]
        if s < SUBLANES:
            mats.append(np.where(right[:, None], (t > mid[:, None]) & (t <= r[:, None]),
                                 (t > r[:, None]) & (t <= mid[:, None])))
        masks.append((blk[:, None] == blk[None, :]) & right[:, None] & (~right)[None, :])
    wmat = np.concatenate(mats, axis=0).astype(np.float32)
    if len(masks) % 2:
        masks.append(np.zeros((c, c), bool))
    mask = np.stack([np.concatenate(masks[p:p + 2], axis=1) for p in range(0, len(masks), 2)])
    return jnp.asarray(wmat, BF16), jnp.asarray(mask.astype(np.float32), F32)


def _split2(x):
    hi = x.astype(BF16)
    return hi, (x - hi.astype(F32)).astype(BF16)


def _level_exponent(cum, s):
    pieces = []
    for r0 in range(0, CHUNK, SUBLANES):
        mid = r0 // (2 * s) * 2 * s + s - 1
        rows, ref = cum[r0:r0 + SUBLANES], cum[mid:mid + 1]
        pieces.append(rows - ref if r0 % (2 * s) >= s else ref - rows)
    return jnp.concatenate(pieces, axis=0)


def _hgrn_kernel(q_ref, f_ref, i_ref, g_ref, gm_ref, lb_ref, nw_ref, w_ref, m_ref, o_ref,
                 st_ref, qs_ref, ks_ref, ob_ref, qe_ref, kd_ref, dec_ref, el_ref, s_ref, kv_ref):
    @pl.when(pl.program_id(2) == 0)
    def _():
        st_ref[...] = jnp.zeros_like(st_ref)

    c = CHUNK
    t_blk, dk = qs_ref.shape
    n_lvl = len(HG_LEVELS)
    n_chunks = t_blk // c
    lb = lb_ref[...]
    zeros = jnp.zeros((c, dk), BF16)
    state = [st_ref[...]]

    def stage_gates(ci):
        rows = slice(ci * c, (ci + 1) * c)
        z = f_ref[rows, :].astype(F32)
        qs = _silu(q_ref[rows, :].astype(F32))
        sig_pos, sig_neg = _sigmoid_pair(z)
        ks = (1.0 - lb) * sig_neg
        qs_ref[rows, :] = qs.astype(BF16)
        ks_ref[rows, :] = ks.astype(BF16)
        hi, lo = _split2(jnp.log(lb + (1.0 - lb) * sig_pos) * LOG2_E)
        g2 = _dot(w_ref[...], jnp.concatenate([hi, lo], axis=-1))
        g = g2[:, :dk] + g2[:, dk:]
        cum = g[0:c]
        e_start = jnp.exp2(cum)
        qe_ref[rows, :] = (qs * e_start).astype(BF16)
        kd_ref[rows, :] = (ks * jnp.exp2(cum[c - 1:c] - cum)).astype(BF16)
        dec_ref[ci] = e_start[c - 1:c]
        narrow = 1
        for l, s in enumerate(HG_LEVELS):
            if s >= SUBLANES:
                expo = _level_exponent(cum, s)
            else:
                expo = g[narrow * c:(narrow + 1) * c]
                narrow += 1
            el_ref[(ci * n_lvl + l) * c:(ci * n_lvl + l + 1) * c, :] = jnp.exp2(expo).astype(BF16)

    def stage_scores(ci):
        rows = slice(ci * c, (ci + 1) * c)
        qb, kb = qs_ref[rows, :], ks_ref[rows, :]
        lhs, rhs = [qb], [kb]
        for l in range(n_lvl):
            el = el_ref[(ci * n_lvl + l) * c:(ci * n_lvl + l + 1) * c, :]
            lhs.append(qb * el)
            rhs.append(kb * el)
        s = None
        for p in range(m_ref.shape[0]):
            a0, b0 = lhs[2 * p], rhs[2 * p]
            if 2 * p + 1 < len(lhs):
                a = jnp.concatenate([a0, lhs[2 * p + 1]], axis=1)
                b = jnp.concatenate([jnp.concatenate([b0, zeros], axis=1),
                                     jnp.concatenate([zeros, rhs[2 * p + 1]], axis=1)], axis=0)
            else:
                a, b = a0, jnp.concatenate([b0, zeros], axis=0)
            tile = m_ref[p] * _dot_nt(a, b)
            s = tile if s is None else s + tile
        s_ref[rows, :] = s.astype(BF16)

    def stage_intra(ci):
        rows = slice(ci * c, (ci + 1) * c)
        vc = i_ref[rows, :]
        ob_ref[rows, :] = _dot(s_ref[rows, :], jnp.concatenate([vc, vc], axis=0))
        kv_ref[ci] = _dot_tn(vc, kd_ref[rows, :])

    def stage_state(ci):
        rows = slice(ci * c, (ci + 1) * c)
        st = state[0]
        o = ob_ref[rows, :] + _dot_nt(qe_ref[rows, :], st.astype(BF16))
        state[0] = st * dec_ref[ci] + kv_ref[ci]
        o = o * lax.rsqrt(jnp.mean(o * o, axis=-1, keepdims=True) + EPS)
        o = o * nw_ref[...] * _silu(g_ref[rows, :].astype(F32))
        o_ref[rows, :] = (_sigmoid(gm_ref[rows, :].astype(F32)) * o).astype(o_ref.dtype)

    stages = (stage_gates, stage_scores, stage_intra, stage_state)
    for step in range(n_chunks + len(stages) - 1):
        for depth, stage in enumerate(stages):
            if 0 <= step - depth < n_chunks:
                stage(step - depth)
    st_ref[...] = state[0]


def _hgrn2(z, lb, hg_norm_w, batch, seq, t_blk=4096):
    n = batch * seq
    d = hg_norm_w.shape[0]
    dk = d // HG_HEADS
    t_blk = min(t_blk, seq)
    nt = seq // t_blk
    wmat, mask = _hgrn_level_tables()

    def part(p):
        return pl.BlockSpec((None, t_blk, LANES), lambda b, h, t, p=p: (p * HG_HEADS + h, b * nt + t, 0))

    head_vec = pl.BlockSpec((1, dk), lambda b, h, t: (0, h))
    return pl.pallas_call(
        _hgrn_kernel,
        out_shape=jax.ShapeDtypeStruct((d // LANES, n, LANES), BF16),
        grid=(batch, HG_HEADS, nt),
        in_specs=[
            part(P_HQ), part(P_HF), part(P_HI), part(P_HG), part(P_GH), head_vec, head_vec,
            pl.BlockSpec(wmat.shape, lambda b, h, t: (0, 0)),
            pl.BlockSpec(mask.shape, lambda b, h, t: (0, 0, 0)),
        ],
        out_specs=pl.BlockSpec((None, t_blk, LANES), lambda b, h, t: (h, b * nt + t, 0)),
        scratch_shapes=[
            pltpu.VMEM((dk, dk), F32), pltpu.VMEM((t_blk, dk), BF16), pltpu.VMEM((t_blk, dk), BF16),
            pltpu.VMEM((t_blk, dk), F32),
            pltpu.VMEM((t_blk, dk), BF16), pltpu.VMEM((t_blk, dk), BF16),
            pltpu.VMEM((t_blk // CHUNK, 1, dk), F32),
            pltpu.VMEM((t_blk * len(HG_LEVELS), dk), BF16), pltpu.VMEM((t_blk, 2 * CHUNK), BF16),
            pltpu.VMEM((t_blk // CHUNK, dk, dk), F32),
        ],
        compiler_params=pltpu.CompilerParams(
            dimension_semantics=("parallel", "parallel", "arbitrary"), vmem_limit_bytes=VMEM_LIMIT),
        name="hgrn2",
    )(z, z, z, z, z, lb.reshape(1, d), hg_norm_w.reshape(1, d), wmat, mask)


def _outproj_kernel(a_ref, b_ref, x_ref, w_ref, nw_ref, wr_ref, br_ref, h_ref, hn_ref, gate_ref, idx_ref):
    merged = jnp.concatenate([(a_ref[g].astype(F32) + b_ref[g].astype(F32)).astype(BF16)
                              for g in range(a_ref.shape[0])], axis=1)
    h = x_ref[...] + _dot(merged, w_ref[...])
    h_ref[...] = h
    hn = h * lax.rsqrt(jnp.mean(h * h, axis=-1, keepdims=True) + EPS) * nw_ref[...]
    hn_ref[...] = hn
    n_e = br_ref.shape[-1]
    hn_hi, hn_lo = _split2(hn)
    l2 = _dot(hn_hi, wr_ref[...])
    logits = l2[:, :n_e] + l2[:, n_e:] + _dot(hn_lo, wr_ref[:, :n_e]) + br_ref[...]
    lane = lax.broadcasted_iota(jnp.int32, logits.shape, 1).astype(F32)
    vals, idxs = [], []
    for _ in range(TOP_K):
        m = jnp.max(logits, axis=-1, keepdims=True)
        idx = jnp.min(jnp.where(logits == m, lane, float(n_e)), axis=-1, keepdims=True)
        vals.append(m)
        idxs.append(idx)
        logits = jnp.where(lane == idx, -jnp.inf, logits)
    exps = [jnp.exp(v - vals[0]) for v in vals]
    denom = exps[0] + exps[1] + exps[2] + exps[3]
    for k in range(TOP_K):
        gate_ref[:, k:k + 1] = exps[k] / denom
        idx_ref[:, k:k + 1] = idxs[k].astype(jnp.int32)


def _outproj(o_r, o_h, x2, w_bf, norm_w, w_router, b_router, tm=512):
    n, d = x2.shape
    n_e = w_router.shape[1]
    row = lambda w: pl.BlockSpec((tm, w), lambda i: (i, 0))
    full = lambda a: pl.BlockSpec(a.shape, lambda i: (0, 0))
    slabs = pl.BlockSpec((d // LANES, tm, LANES), lambda i: (0, i, 0))
    nw, br = norm_w.reshape(1, d), b_router.reshape(1, n_e)
    wr = jnp.concatenate(_split2(w_router), axis=1)
    return pl.pallas_call(
        _outproj_kernel,
        out_shape=(jax.ShapeDtypeStruct((n, d), F32), jax.ShapeDtypeStruct((n, d), F32),
                   jax.ShapeDtypeStruct((n, TOP_K), F32), jax.ShapeDtypeStruct((n, TOP_K), jnp.int32)),
        grid=(n // tm,),
        in_specs=[slabs, slabs, row(d), full(w_bf), full(nw), full(wr), full(br)],
        out_specs=(row(d), row(d), row(TOP_K), row(TOP_K)),
        compiler_params=pltpu.CompilerParams(
            dimension_semantics=("parallel",), vmem_limit_bytes=VMEM_LIMIT),
        name="outproj_router",
    )(o_r, o_h, x2, w_bf, nw, wr, br)


def _rank_kernel(idx_ref, tri_ref, rank_ref, cnt_ref, run_ref):
    @pl.when(pl.program_id(0) == 0)
    def _():
        run_ref[...] = jnp.zeros_like(run_ref)

    idx = idx_ref[...]
    lane = lax.broadcasted_iota(jnp.int32, (idx.shape[0], N_EXPERTS), 1)
    run = run_ref[...]
    for k in range(TOP_K):
        onehot = lane == idx[:, k:k + 1]
        before = _dot(tri_ref[...], onehot.astype(BF16)) + run
        rank_ref[:, k:k + 1] = jnp.sum(jnp.where(onehot, before, 0.0), axis=-1, keepdims=True).astype(jnp.int32)
        run = run + jnp.sum(onehot.astype(F32), axis=0, keepdims=True)
    run_ref[...] = run
    cnt_ref[...] = run.astype(jnp.int32)


def _pair_ranks(idx, blk=512):
    n = idx.shape[0]
    r = np.arange(blk)
    tri = jnp.asarray(r[None, :] < r[:, None], BF16)
    return pl.pallas_call(
        _rank_kernel,
        out_shape=(jax.ShapeDtypeStruct((n, TOP_K), jnp.int32), jax.ShapeDtypeStruct((1, N_EXPERTS), jnp.int32)),
        grid=(n // blk,),
        in_specs=[pl.BlockSpec((blk, TOP_K), lambda i: (i, 0)), pl.BlockSpec((blk, blk), lambda i: (0, 0))],
        out_specs=(pl.BlockSpec((blk, TOP_K), lambda i: (i, 0)), pl.BlockSpec((1, N_EXPERTS), lambda i: (0, 0))),
        scratch_shapes=[pltpu.VMEM((1, N_EXPERTS), F32)],
        compiler_params=pltpu.CompilerParams(dimension_semantics=("arbitrary",)),
        name="pair_ranks",
    )(idx, tri)


def _dispatch_kernel(dest_ref, src_ref, dst_ref, sem, *, tc):
    base = pl.program_id(0) * tc * TOP_K

    def issue(tile, carry):
        src_tile = src_ref.at[pl.ds(pl.multiple_of(tile * SUBLANES, SUBLANES), SUBLANES)]
        for u in range(SUBLANES):
            for k in range(TOP_K):
                pair = base + (tile * SUBLANES + u) * TOP_K + k
                pltpu.make_async_copy(src_tile.at[pl.ds(u, 1)], dst_ref.at[pl.ds(dest_ref[pair], 1)],
                                      sem).start(priority=k % 2)
        return carry

    lax.fori_loop(0, tc // SUBLANES, issue, 0)
    for _ in range(TOP_K):
        pltpu.make_async_copy(src_ref, dst_ref.at[pl.ds(0, tc)], sem).wait()


def _dispatch_rows(src, dest, m_pad, tc=256):
    n, d = src.shape
    return pl.pallas_call(
        functools.partial(_dispatch_kernel, tc=tc),
        out_shape=jax.ShapeDtypeStruct((m_pad, d), src.dtype),
        grid_spec=pltpu.PrefetchScalarGridSpec(
            num_scalar_prefetch=1, grid=(n // tc,),
            in_specs=[pl.BlockSpec((tc, d), lambda i, dref: (i, 0))],
            out_specs=pl.BlockSpec(memory_space=pl.ANY),
            scratch_shapes=[pltpu.SemaphoreType.DMA]),
        compiler_params=pltpu.CompilerParams(
            dimension_semantics=("arbitrary",), vmem_limit_bytes=VMEM_LIMIT),
        name="dispatch_rows",
    )(dest, src)


PAIR = 2 * LANES
SUB_ROWS = 128
SPLIT_ROWS = 512


def _even_lane_selector():
    c = np.arange(LANES)
    blk = np.zeros((LANES, LANES), np.float32)
    blk[c, c // 2 + LANES // 2 * (c % 2)] = 1.0
    return jnp.asarray(np.kron(np.eye(2, dtype=np.float32), blk), BF16)


def _expert_kernel(be_ref, nv_ref, nb_ref, x_ref, w1_ref, b1_ref, w2_ref, b2_ref, sel_ref, y_ref,
                   xb_ref):
    i, j = pl.program_id(0), pl.program_id(1)
    n_valid = nv_ref[i]
    tm = xb_ref.shape[0]

    @pl.when(i < nb_ref[0])
    def _():
        @pl.when(j == 0)
        def _():
            row = lax.broadcasted_iota(jnp.int32, (tm, 1), 0)
            xb_ref[...] = jnp.where(row < n_valid, x_ref[...], 0.0).astype(BF16)

        def rows_pass(start_row, n_rows):
            align = min(SUB_ROWS, tm)
            halves = 2 if n_rows >= SPLIT_ROWS else 1
            hr = n_rows // halves
            even = lax.broadcasted_iota(jnp.int32, (hr, LANES), 1) % 2 == 0
            w1b, w2b = w1_ref[...].astype(BF16), w2_ref[...].astype(BF16)
            hcats = []
            for p in range(halves):
                row0 = pl.multiple_of(start_row + p * hr, align)
                hcats.append((row0, _dot(xb_ref[pl.ds(row0, hr), :], w1b) + b1_ref[...]))
            for row0, hcat in hcats:
                acts = []
                for g in range(hcat.shape[1] // PAIR):
                    h0, h1 = hcat[:, g * PAIR:g * PAIR + LANES], hcat[:, g * PAIR + LANES:(g + 1) * PAIR]
                    glu = jnp.where(even, h0, pltpu.roll(h1, 1, 1))
                    lin = jnp.where(even, pltpu.roll(h0, LANES - 1, 1), h1)
                    glu = jnp.minimum(glu, SWIGLU_LIMIT)
                    lin = jnp.clip(lin, -SWIGLU_LIMIT, SWIGLU_LIMIT)
                    acts.append((glu * _sigmoid(glu, SWIGLU_ALPHA) * (lin + 1.0)).astype(BF16))
                hid = []
                for g in range(0, len(acts), 2):
                    width = LANES * len(acts[g:g + 2])
                    hid.append(_dot(jnp.concatenate(acts[g:g + 2], axis=1),
                                    sel_ref[0:width, 0:width]).astype(BF16))
                part = _dot(jnp.concatenate(hid, axis=1), w2b)
                rows = pl.ds(row0, hr)
                start = jnp.where(j == 0, jnp.broadcast_to(b2_ref[...], part.shape), y_ref[rows, :])
                y_ref[rows, :] = start + part

        unit = min(SUB_ROWS, tm)
        n_units = (n_valid + unit - 1) // unit
        for bit in reversed(range((tm // unit).bit_length())):
            @pl.when(((n_units >> bit) & 1) == 1)
            def _(bit=bit):
                rows_pass((n_units >> (bit + 1) << (bit + 1)) * unit, unit << bit)


def _experts(xs, block_expert, block_valid, n_blocks_used, w1, b1, w2, b2, tm, tf=256):
    m_pad, d = xs.shape
    f = w2.shape[1]
    tf = min(tf, f)
    nf = f // tf
    nblk = m_pad // tm
    sel = _even_lane_selector()

    def blk(i, nb):
        return jnp.minimum(i, nb[0] - 1)

    def ftile(i, j, nb):
        return jnp.where(i < nb[0], j, nf - 1)

    return pl.pallas_call(
        _expert_kernel,
        out_shape=jax.ShapeDtypeStruct((m_pad, d), F32),
        grid_spec=pltpu.PrefetchScalarGridSpec(
            num_scalar_prefetch=3, grid=(nblk, nf),
            in_specs=[
                pl.BlockSpec((tm, d), lambda i, j, be, nv, nb: (blk(i, nb), 0)),
                pl.BlockSpec((None, d, 2 * tf), lambda i, j, be, nv, nb: (be[blk(i, nb)], 0, ftile(i, j, nb))),
                pl.BlockSpec((None, 1, 2 * tf), lambda i, j, be, nv, nb: (be[blk(i, nb)], 0, ftile(i, j, nb))),
                pl.BlockSpec((None, tf, d), lambda i, j, be, nv, nb: (be[blk(i, nb)], ftile(i, j, nb), 0)),
                pl.BlockSpec((None, 1, d), lambda i, j, be, nv, nb: (be[blk(i, nb)], 0, 0)),
                pl.BlockSpec(sel.shape, lambda i, j, be, nv, nb: (0, 0)),
            ],
            out_specs=pl.BlockSpec((tm, d), lambda i, j, be, nv, nb: (blk(i, nb), 0)),
            scratch_shapes=[pltpu.VMEM((tm, d), BF16)]),
        compiler_params=pltpu.CompilerParams(
            dimension_semantics=("arbitrary", "arbitrary"), vmem_limit_bytes=VMEM_LIMIT),
        name="experts",
    )(block_expert, block_valid, n_blocks_used, xs, w1, b1, w2, b2, sel)


def _combine_kernel(dest_ref, h_ref, gate_ref, nw_ref, y_ref, o_ref, buf_ref, sem, *, tc):
    i = pl.program_id(0)
    slot = i % 2

    def fetch(block, into):
        base = block * tc * TOP_K

        def issue(tile, carry):
            tile_rows = pl.ds(pl.multiple_of(tile * SUBLANES, SUBLANES), SUBLANES)
            for k in range(TOP_K):
                dst_tile = buf_ref.at[into, k, tile_rows]
                for u in range(SUBLANES):
                    pair = base + (tile * SUBLANES + u) * TOP_K + k
                    pltpu.make_async_copy(y_ref.at[pl.ds(dest_ref[pair], 1)], dst_tile.at[pl.ds(u, 1)],
                                          sem.at[into]).start(priority=k % 2)
            return carry

        lax.fori_loop(0, tc // SUBLANES, issue, 0)

    @pl.when(i == 0)
    def _():
        fetch(0, 0)

    @pl.when(i + 1 < pl.num_programs(0))
    def _():
        fetch(i + 1, 1 - slot)

    for k in range(TOP_K):
        pltpu.make_async_copy(y_ref.at[pl.ds(0, tc)], buf_ref.at[slot, k], sem.at[slot]).wait()
    gate = gate_ref[...]
    out = h_ref[...]
    for k in range(TOP_K):
        out = out + gate[:, k:k + 1] * buf_ref[slot, k]
    out = out * lax.rsqrt(jnp.mean(out * out, axis=-1, keepdims=True) + EPS) * nw_ref[...]
    o_ref[...] = out


def _combine(dest, h, gates, norm_w, y, tc=256):
    n, d = h.shape
    return pl.pallas_call(
        functools.partial(_combine_kernel, tc=tc),
        out_shape=jax.ShapeDtypeStruct((n, d), F32),
        grid_spec=pltpu.PrefetchScalarGridSpec(
            num_scalar_prefetch=1, grid=(n // tc,),
            in_specs=[
                pl.BlockSpec((tc, d), lambda i, dref: (i, 0)),
                pl.BlockSpec((tc, TOP_K), lambda i, dref: (i, 0)),
                pl.BlockSpec((1, d), lambda i, dref: (0, 0)),
                pl.BlockSpec(memory_space=pl.ANY),
            ],
            out_specs=pl.BlockSpec((tc, d), lambda i, dref: (i, 0)),
            scratch_shapes=[pltpu.VMEM((2, TOP_K, tc, d), F32), pltpu.SemaphoreType.DMA((2,))]),
        compiler_params=pltpu.CompilerParams(
            dimension_semantics=("arbitrary",), vmem_limit_bytes=VMEM_LIMIT),
        name="combine_norm",
    )(dest, h, gates, norm_w.reshape(1, d), y)


def _moe(h, hn, gates, idx, w1, b1, w2, b2, norm_final_w, tm=1024, tf=256):
    n, d = h.shape
    n_pairs = n * TOP_K
    e_flat = idx.reshape(n_pairs)
    rank, counts = _pair_ranks(idx, blk=min(512, n))
    counts = counts.reshape(N_EXPERTS)
    padded = (counts + tm - 1) // tm * tm
    end_pad = jnp.cumsum(padded)
    start_pad = end_pad - padded
    dest = start_pad[e_flat] + rank.reshape(n_pairs)
    dest = dest.astype(jnp.int32)
    m_pad = -(-(n_pairs + N_EXPERTS * tm) // tm) * tm
    n_blocks = m_pad // tm
    block_start = jnp.arange(n_blocks, dtype=jnp.int32) * tm
    block_expert = jnp.minimum(jnp.sum(end_pad[None, :] <= block_start[:, None], axis=1),
                               N_EXPERTS - 1).astype(jnp.int32)
    block_valid = jnp.clip((start_pad + counts)[block_expert] - block_start, 0, tm).astype(jnp.int32)
    n_blocks_used = (end_pad[-1:] // tm).astype(jnp.int32)
    xs = _dispatch_rows(hn, dest, m_pad, tc=min(512, n))
    y = _experts(xs, block_expert, block_valid, n_blocks_used, w1, b1[:, None, :], w2, b2[:, None, :], tm, tf)
    return _combine(dest, h, gates, norm_final_w, y)


def kernel(x, w_in, ret_norm_w, hg_norm_w, hg_lb_logits, w_out, norm_mix_w, norm_moe_w, w_router, b_router,
           w1, b1, w2, b2, norm_final_w):
    batch, seq, d = x.shape
    depth = w_in.shape[0]
    assert depth == 1, "the final norm is fused into the last layer's combine; one layer supported"
    lb_all = jnp.cumsum(jax.nn.softmax(hg_lb_logits.astype(F32), axis=0), axis=0)
    x2 = x.reshape(batch * seq, d)
    z = _inproj(x2, norm_mix_w[0], w_in[0])
    o_r = _retention(z, ret_norm_w[0], batch, seq)
    o_h = _hgrn2(z, lb_all[0], hg_norm_w[0], batch, seq)
    h, hn, gates, idx = _outproj(o_r, o_h, x2, w_out[0].astype(BF16), norm_moe_w[0], w_router[0], b_router[0])
    out = _moe(h, hn, gates, idx, w1[0], b1[0], w2[0], b2[0], norm_final_w)
    return out.reshape(batch, seq, d)
```

```python
import functools

import numpy as np
import jax
import jax.numpy as jnp
from jax import lax
from jax.experimental import pallas as pl
from jax.experimental.pallas import tpu as pltpu

F32 = jnp.float32
BF16 = jnp.bfloat16

LANES = 128
SUBLANES = 8
CHUNK = 64
HG_LEVELS = (32, 16, 8, 4, 2, 1)
RET_HEADS = 8
HG_HEADS = 16
N_SPLITS = 10
N_EXPERTS = 32
TOP_K = 4
SWIGLU_ALPHA = 1.702
SWIGLU_LIMIT = 7.0
ROPE_BASE = 10000.0
EPS = 1e-6
LOG2_E = 1.4426950408889634
VMEM_LIMIT = 56 * 1024 * 1024

P_RQ, P_RK, P_RV, P_RG, P_HQ, P_HF, P_HI, P_HG, P_GR, P_GH = range(N_SPLITS)


def _sigmoid(x, scale=1.0):
    return 1.0 / (1.0 + jnp.exp2(x * (-scale * LOG2_E)))


def _sigmoid_pair(x):
    t = jnp.exp(-jnp.abs(x))
    r = 1.0 / (1.0 + t)
    p = t * r
    pos = x >= 0.0
    return jnp.where(pos, r, p), jnp.where(pos, p, r)


def _silu(x):
    return x * _sigmoid(x)


def _dot(a, b):
    return jnp.dot(a, b, preferred_element_type=F32)


def _dot_nt(a, b):
    return lax.dot_general(a, b, (((1,), (1,)), ((), ())), preferred_element_type=F32)


def _dot_tn(a, b):
    return lax.dot_general(a, b, (((0,), (0,)), ((), ())), preferred_element_type=F32)


def _inproj_kernel(x_ref, nw_ref, w_ref, o_ref, xn_ref):
    @pl.when(pl.program_id(1) == 0)
    def _():
        x = x_ref[...]
        ms = jnp.mean(x * x, axis=-1, keepdims=True)
        xn_ref[...] = (x * lax.rsqrt(ms + EPS) * nw_ref[...]).astype(BF16)

    acc = _dot(xn_ref[...], w_ref[...].astype(BF16))
    for g in range(o_ref.shape[0]):
        o_ref[g] = acc[:, g * LANES:(g + 1) * LANES].astype(o_ref.dtype)


def _inproj(x2, norm_w, w_bf, tm=1024, tn=1024):
    n, d = x2.shape
    cols = w_bf.shape[1]
    gpt = tn // LANES
    return pl.pallas_call(
        _inproj_kernel,
        out_shape=jax.ShapeDtypeStruct((cols // LANES, n, LANES), BF16),
        grid=(n // tm, cols // tn),
        in_specs=[
            pl.BlockSpec((tm, d), lambda i, j: (i, 0)),
            pl.BlockSpec((1, d), lambda i, j: (0, 0)),
            pl.BlockSpec((d, tn), lambda i, j: (0, j)),
        ],
        out_specs=pl.BlockSpec((gpt, tm, LANES), lambda i, j: (j, i, 0)),
        scratch_shapes=[pltpu.VMEM((tm, d), BF16)],
        compiler_params=pltpu.CompilerParams(
            dimension_semantics=("parallel", "arbitrary"), vmem_limit_bytes=VMEM_LIMIT),
        name="inproj",
    )(x2, norm_w.reshape(1, d), w_bf)


def _retention_tables(seq, t_blk, dk):
    h = RET_HEADS
    log_gamma = jnp.log1p(-jnp.exp2(-5.0 - jnp.arange(h, dtype=F32)))
    inv = 1.0 / (ROPE_BASE ** jnp.linspace(0.0, 1.0, dk // 2, dtype=F32))
    ang = jnp.arange(seq, dtype=F32)[:, None] * inv[None, :]
    cos, sin = jnp.cos(ang), jnp.sin(ang)
    pos = np.arange(t_blk)
    cq, ck = pos[:, None] // CHUNK, pos[None, :] // CHUNK
    dist = (pos[:, None] - pos[None, :]).astype(np.float32)
    expo = np.where(cq == ck, np.abs(dist), dist)
    allowed = jnp.asarray(cq >= ck)
    dmask = jnp.where(allowed[None], jnp.exp(jnp.asarray(expo)[None] * log_gamma[:, None, None]), 0.0)
    posf = jnp.arange(t_blk, dtype=F32)
    qdec = jnp.exp((posf[None, :] + 1.0) * log_gamma[:, None])[..., None]
    kdec = jnp.exp((t_blk - 1.0 - posf)[None, :] * log_gamma[:, None])[..., None]
    sdec = jnp.broadcast_to(jnp.exp(t_blk * log_gamma)[:, None, None], (h, 1, LANES))
    return cos, sin, dmask, qdec, kdec, sdec


def _retention_kernel(q_ref, k_ref, v_ref, g_ref, gm_ref, cos_ref, sin_ref, dm_ref, qd_ref, kd_ref,
                      sd_ref, nw_ref, o_ref, st_ref, *, scale):
    @pl.when(pl.program_id(2) == 0)
    def _():
        st_ref[...] = jnp.zeros_like(st_ref)

    cos, sin = cos_ref[...], sin_ref[...]
    heads = st_ref.shape[0]
    gph = q_ref.shape[0] // heads
    dk = gph * LANES

    def slabs(ref, hd):
        return [ref[hd * gph + g] for g in range(gph)]

    def rot(ref, hd):
        x1, x2 = [s.astype(F32) for s in slabs(ref, hd)]
        return jnp.concatenate([x1 * cos - x2 * sin, x2 * cos + x1 * sin], axis=-1)

    def stage_rotate(hd):
        q = rot(q_ref, hd)
        k = rot(k_ref, hd) * scale
        return (q.astype(BF16), k.astype(BF16), (q * qd_ref[hd]).astype(BF16), (k * kd_ref[hd]).astype(BF16))

    def stage_matmuls(hd, rotated):
        q_bf, k_bf, q_dec, k_dec = rotated
        v = jnp.concatenate(slabs(v_ref, hd), axis=-1)
        scores = _dot_nt(q_bf, k_bf) * dm_ref[hd]
        state = st_ref[hd]
        o = _dot(scores.astype(BF16), v) + _dot(q_dec, state.astype(BF16))
        st_ref[hd] = sd_ref[hd][:, 0:1] * state + _dot_tn(k_dec, v)
        return o

    def stage_norm(hd, o):
        o = o - jnp.mean(o, axis=-1, keepdims=True)
        o = o * lax.rsqrt(jnp.mean(o * o, axis=-1, keepdims=True) + EPS)
        gate = jnp.concatenate(slabs(g_ref, hd), axis=-1).astype(F32)
        merge = jnp.concatenate(slabs(gm_ref, hd), axis=-1).astype(F32)
        res = _sigmoid(merge) * (o * nw_ref[:, hd * dk:(hd + 1) * dk] * _silu(gate))
        for g in range(gph):
            o_ref[hd * gph + g] = res[:, g * LANES:(g + 1) * LANES].astype(o_ref.dtype)

    rotated = [stage_rotate(hd) for hd in range(heads)]
    outs = [stage_matmuls(hd, rotated[hd]) for hd in range(heads)]
    for hd in range(heads):
        stage_norm(hd, outs[hd])


def _retention(z, ret_norm_w, batch, seq, t_blk=512, heads=4):
    n = batch * seq
    d = ret_norm_w.shape[0]
    dk = d // RET_HEADS
    gps = heads * dk // LANES
    nt = seq // t_blk
    cos, sin, dmask, qdec, kdec, sdec = _retention_tables(seq, t_blk, dk)

    def part(p):
        return pl.BlockSpec((gps, t_blk, LANES), lambda b, h, t, p=p: (p * RET_HEADS // heads + h, b * nt + t, 0))

    per_head = lambda shape: pl.BlockSpec((heads,) + shape, lambda b, h, t: (h, 0, 0))
    return pl.pallas_call(
        functools.partial(_retention_kernel, scale=dk ** -0.5),
        out_shape=jax.ShapeDtypeStruct((d // LANES, n, LANES), BF16),
        grid=(batch, RET_HEADS // heads, nt),
        in_specs=[
            part(P_RQ), part(P_RK), part(P_RV), part(P_RG), part(P_GR),
            pl.BlockSpec((t_blk, dk // 2), lambda b, h, t: (t, 0)),
            pl.BlockSpec((t_blk, dk // 2), lambda b, h, t: (t, 0)),
            per_head((t_blk, t_blk)), per_head((t_blk, 1)), per_head((t_blk, 1)), per_head((1, LANES)),
            pl.BlockSpec((1, heads * dk), lambda b, h, t: (0, h)),
        ],
        out_specs=pl.BlockSpec((gps, t_blk, LANES), lambda b, h, t: (h, b * nt + t, 0)),
        scratch_shapes=[pltpu.VMEM((heads, dk, dk), F32)],
        compiler_params=pltpu.CompilerParams(
            dimension_semantics=("parallel", "parallel", "arbitrary"), vmem_limit_bytes=VMEM_LIMIT),
        name="retention",
    )(z, z, z, z, z, cos, sin, dmask, qdec, kdec, sdec, ret_norm_w.reshape(1, d))


def _hgrn_level_tables():
    c = CHUNK
    r = np.arange(c)
    mats = [(r[None, :] <= r[:, None])]
    masks = [np.eye(c, dtype=bool)]
    for s in HG_LEVELS:
        blk, within = r // (2 * s), r % (2 * s)
        mid = blk * 2 * s + s - 1
        right = within >= s
        t = r[None, :]
        if s < SUBLANES:
            mats.append(np.where(right[:, None], (t > mid[:, None]) & (t <= r[:, None]),
                                 (t > r[:, None]) & (t <= mid[:, None])))
        masks.append((blk[:, None] == blk[None, :]) & right[:, None] & (~right)[None, :])
    wmat = np.concatenate(mats, axis=0).astype(np.float32)
    if len(masks) % 2:
        masks.append(np.zeros((c, c), bool))
    mask = np.stack([np.concatenate(masks[p:p + 2], axis=1) for p in range(0, len(masks), 2)])
    return jnp.asarray(wmat, BF16), jnp.asarray(mask.astype(np.float32), F32)


def _split2(x):
    hi = x.astype(BF16)
    return hi, (x - hi.astype(F32)).astype(BF16)


def _level_exponent(cum, s):
    pieces = []
    for r0 in range(0, CHUNK, SUBLANES):
        mid = r0 // (2 * s) * 2 * s + s - 1
        rows, ref = cum[r0:r0 + SUBLANES], cum[mid:mid + 1]
        pieces.append(rows - ref if r0 % (2 * s) >= s else ref - rows)
    return jnp.concatenate(pieces, axis=0)


def _hgrn_kernel(q_ref, f_ref, i_ref, g_ref, gm_ref, lb_ref, nw_ref, w_ref, m_ref, o_ref,
                 st_ref, qs_ref, ks_ref, ob_ref, qe_ref, kd_ref, dec_ref, el_ref, s_ref, kv_ref):
    @pl.when(pl.program_id(2) == 0)
    def _():
        st_ref[...] = jnp.zeros_like(st_ref)

    c = CHUNK
    t_blk, dk = qs_ref.shape
    n_lvl = len(HG_LEVELS)
    n_chunks = t_blk // c
    lb = lb_ref[...]
    zeros = jnp.zeros((c, dk), BF16)
    state = [st_ref[...]]

    def stage_gates(ci):
        rows = slice(ci * c, (ci + 1) * c)
        z = f_ref[rows, :].astype(F32)
        qs = _silu(q_ref[rows, :].astype(F32))
        sig_pos, sig_neg = _sigmoid_pair(z)
        ks = (1.0 - lb) * sig_neg
        qs_ref[rows, :] = qs.astype(BF16)
        ks_ref[rows, :] = ks.astype(BF16)
        hi, lo = _split2(jnp.log(lb + (1.0 - lb) * sig_pos) * LOG2_E)
        g2 = _dot(w_ref[...], jnp.concatenate([hi, lo], axis=-1))
        g = g2[:, :dk] + g2[:, dk:]
        cum = g[0:c]
        e_start = jnp.exp2(cum)
        qe_ref[rows, :] = (qs * e_start).astype(BF16)
        kd_ref[rows, :] = (ks * jnp.exp2(cum[c - 1:c] - cum)).astype(BF16)
        dec_ref[ci] = e_start[c - 1:c]
        narrow = 1
        for l, s in enumerate(HG_LEVELS):
            if s >= SUBLANES:
                expo = _level_exponent(cum, s)
            else:
                expo = g[narrow * c:(narrow + 1) * c]
                narrow += 1
            el_ref[(ci * n_lvl + l) * c:(ci * n_lvl + l + 1) * c, :] = jnp.exp2(expo).astype(BF16)

    def stage_scores(ci):
        rows = slice(ci * c, (ci + 1) * c)
        qb, kb = qs_ref[rows, :], ks_ref[rows, :]
        lhs, rhs = [qb], [kb]
        for l in range(n_lvl):
            el = el_ref[(ci * n_lvl + l) * c:(ci * n_lvl + l + 1) * c, :]
            lhs.append(qb * el)
            rhs.append(kb * el)
        s = None
        for p in range(m_ref.shape[0]):
            a0, b0 = lhs[2 * p], rhs[2 * p]
            if 2 * p + 1 < len(lhs):
                a = jnp.concatenate([a0, lhs[2 * p + 1]], axis=1)
                b = jnp.concatenate([jnp.concatenate([b0, zeros], axis=1),
                                     jnp.concatenate([zeros, rhs[2 * p + 1]], axis=1)], axis=0)
            else:
                a, b = a0, jnp.concatenate([b0, zeros], axis=0)
            tile = m_ref[p] * _dot_nt(a, b)
            s = tile if s is None else s + tile
        s_ref[rows, :] = s.astype(BF16)

    def stage_intra(ci):
        rows = slice(ci * c, (ci + 1) * c)
        vc = i_ref[rows, :]
        ob_ref[rows, :] = _dot(s_ref[rows, :], jnp.concatenate([vc, vc], axis=0))
        kv_ref[ci] = _dot_tn(vc, kd_ref[rows, :])

    def stage_state(ci):
        rows = slice(ci * c, (ci + 1) * c)
        st = state[0]
        o = ob_ref[rows, :] + _dot_nt(qe_ref[rows, :], st.astype(BF16))
        state[0] = st * dec_ref[ci] + kv_ref[ci]
        o = o * lax.rsqrt(jnp.mean(o * o, axis=-1, keepdims=True) + EPS)
        o = o * nw_ref[...] * _silu(g_ref[rows, :].astype(F32))
        o_ref[rows, :] = (_sigmoid(gm_ref[rows, :].astype(F32)) * o).astype(o_ref.dtype)

    stages = (stage_gates, stage_scores, stage_intra, stage_state)
    for step in range(n_chunks + len(stages) - 1):
        for depth, stage in enumerate(stages):
            if 0 <= step - depth < n_chunks:
                stage(step - depth)
    st_ref[...] = state[0]


def _hgrn2(z, lb, hg_norm_w, batch, seq, t_blk=4096):
    n = batch * seq
    d = hg_norm_w.shape[0]
    dk = d // HG_HEADS
    t_blk = min(t_blk, seq)
    nt = seq // t_blk
    wmat, mask = _hgrn_level_tables()

    def part(p):
        return pl.BlockSpec((None, t_blk, LANES), lambda b, h, t, p=p: (p * HG_HEADS + h, b * nt + t, 0))

    head_vec = pl.BlockSpec((1, dk), lambda b, h, t: (0, h))
    return pl.pallas_call(
        _hgrn_kernel,
        out_shape=jax.ShapeDtypeStruct((d // LANES, n, LANES), BF16),
        grid=(batch, HG_HEADS, nt),
        in_specs=[
            part(P_HQ), part(P_HF), part(P_HI), part(P_HG), part(P_GH), head_vec, head_vec,
            pl.BlockSpec(wmat.shape, lambda b, h, t: (0, 0)),
            pl.BlockSpec(mask.shape, lambda b, h, t: (0, 0, 0)),
        ],
        out_specs=pl.BlockSpec((None, t_blk, LANES), lambda b, h, t: (h, b * nt + t, 0)),
        scratch_shapes=[
            pltpu.VMEM((dk, dk), F32), pltpu.VMEM((t_blk, dk), BF16), pltpu.VMEM((t_blk, dk), BF16),
            pltpu.VMEM((t_blk, dk), F32),
            pltpu.VMEM((t_blk, dk), BF16), pltpu.VMEM((t_blk, dk), BF16),
            pltpu.VMEM((t_blk // CHUNK, 1, dk), F32),
            pltpu.VMEM((t_blk * len(HG_LEVELS), dk), BF16), pltpu.VMEM((t_blk, 2 * CHUNK), BF16),
            pltpu.VMEM((t_blk // CHUNK, dk, dk), F32),
        ],
        compiler_params=pltpu.CompilerParams(
            dimension_semantics=("parallel", "parallel", "arbitrary"), vmem_limit_bytes=VMEM_LIMIT),
        name="hgrn2",
    )(z, z, z, z, z, lb.reshape(1, d), hg_norm_w.reshape(1, d), wmat, mask)


def _outproj_kernel(a_ref, b_ref, x_ref, w_ref, nw_ref, wr_ref, br_ref, h_ref, hn_ref, gate_ref, idx_ref):
    merged = jnp.concatenate([(a_ref[g].astype(F32) + b_ref[g].astype(F32)).astype(BF16)
                              for g in range(a_ref.shape[0])], axis=1)
    h = x_ref[...] + _dot(merged, w_ref[...])
    h_ref[...] = h
    hn = h * lax.rsqrt(jnp.mean(h * h, axis=-1, keepdims=True) + EPS) * nw_ref[...]
    hn_ref[...] = hn
    n_e = br_ref.shape[-1]
    hn_hi, hn_lo = _split2(hn)
    l2 = _dot(hn_hi, wr_ref[...])
    logits = l2[:, :n_e] + l2[:, n_e:] + _dot(hn_lo, wr_ref[:, :n_e]) + br_ref[...]
    lane = lax.broadcasted_iota(jnp.int32, logits.shape, 1).astype(F32)
    vals, idxs = [], []
    for _ in range(TOP_K):
        m = jnp.max(logits, axis=-1, keepdims=True)
        idx = jnp.min(jnp.where(logits == m, lane, float(n_e)), axis=-1, keepdims=True)
        vals.append(m)
        idxs.append(idx)
        logits = jnp.where(lane == idx, -jnp.inf, logits)
    exps = [jnp.exp(v - vals[0]) for v in vals]
    denom = exps[0] + exps[1] + exps[2] + exps[3]
    for k in range(TOP_K):
        gate_ref[:, k:k + 1] = exps[k] / denom
        idx_ref[:, k:k + 1] = idxs[k].astype(jnp.int32)


def _outproj(o_r, o_h, x2, w_bf, norm_w, w_router, b_router, tm=512):
    n, d = x2.shape
    n_e = w_router.shape[1]
    row = lambda w: pl.BlockSpec((tm, w), lambda i: (i, 0))
    full = lambda a: pl.BlockSpec(a.shape, lambda i: (0, 0))
    slabs = pl.BlockSpec((d // LANES, tm, LANES), lambda i: (0, i, 0))
    nw, br = norm_w.reshape(1, d), b_router.reshape(1, n_e)
    wr = jnp.concatenate(_split2(w_router), axis=1)
    return pl.pallas_call(
        _outproj_kernel,
        out_shape=(jax.ShapeDtypeStruct((n, d), F32), jax.ShapeDtypeStruct((n, d), F32),
                   jax.ShapeDtypeStruct((n, TOP_K), F32), jax.ShapeDtypeStruct((n, TOP_K), jnp.int32)),
        grid=(n // tm,),
        in_specs=[slabs, slabs, row(d), full(w_bf), full(nw), full(wr), full(br)],
        out_specs=(row(d), row(d), row(TOP_K), row(TOP_K)),
        compiler_params=pltpu.CompilerParams(
            dimension_semantics=("parallel",), vmem_limit_bytes=VMEM_LIMIT),
        name="outproj_router",
    )(o_r, o_h, x2, w_bf, nw, wr, br)


def _rank_kernel(idx_ref, tri_ref, rank_ref, cnt_ref, run_ref):
    @pl.when(pl.program_id(0) == 0)
    def _():
        run_ref[...] = jnp.zeros_like(run_ref)

    idx = idx_ref[...]
    lane = lax.broadcasted_iota(jnp.int32, (idx.shape[0], N_EXPERTS), 1)
    run = run_ref[...]
    for k in range(TOP_K):
        onehot = lane == idx[:, k:k + 1]
        before = _dot(tri_ref[...], onehot.astype(BF16)) + run
        rank_ref[:, k:k + 1] = jnp.sum(jnp.where(onehot, before, 0.0), axis=-1, keepdims=True).astype(jnp.int32)
        run = run + jnp.sum(onehot.astype(F32), axis=0, keepdims=True)
    run_ref[...] = run
    cnt_ref[...] = run.astype(jnp.int32)


def _pair_ranks(idx, blk=512):
    n = idx.shape[0]
    r = np.arange(blk)
    tri = jnp.asarray(r[None, :] < r[:, None], BF16)
    return pl.pallas_call(
        _rank_kernel,
        out_shape=(jax.ShapeDtypeStruct((n, TOP_K), jnp.int32), jax.ShapeDtypeStruct((1, N_EXPERTS), jnp.int32)),
        grid=(n // blk,),
        in_specs=[pl.BlockSpec((blk, TOP_K), lambda i: (i, 0)), pl.BlockSpec((blk, blk), lambda i: (0, 0))],
        out_specs=(pl.BlockSpec((blk, TOP_K), lambda i: (i, 0)), pl.BlockSpec((1, N_EXPERTS), lambda i: (0, 0))),
        scratch_shapes=[pltpu.VMEM((1, N_EXPERTS), F32)],
        compiler_params=pltpu.CompilerParams(dimension_semantics=("arbitrary",)),
        name="pair_ranks",
    )(idx, tri)


def _dispatch_kernel(dest_ref, src_ref, dst_ref, sem, *, tc):
    base = pl.program_id(0) * tc * TOP_K

    def issue(tile, carry):
        src_tile = src_ref.at[pl.ds(pl.multiple_of(tile * SUBLANES, SUBLANES), SUBLANES)]
        for u in range(SUBLANES):
            for k in range(TOP_K):
                pair = base + (tile * SUBLANES + u) * TOP_K + k
                pltpu.make_async_copy(src_tile.at[pl.ds(u, 1)], dst_ref.at[pl.ds(dest_ref[pair], 1)],
                                      sem).start(priority=k % 2)
        return carry

    lax.fori_loop(0, tc // SUBLANES, issue, 0)
    for _ in range(TOP_K):
        pltpu.make_async_copy(src_ref, dst_ref.at[pl.ds(0, tc)], sem).wait()


def _dispatch_rows(src, dest, m_pad, tc=256):
    n, d = src.shape
    return pl.pallas_call(
        functools.partial(_dispatch_kernel, tc=tc),
        out_shape=jax.ShapeDtypeStruct((m_pad, d), src.dtype),
        grid_spec=pltpu.PrefetchScalarGridSpec(
            num_scalar_prefetch=1, grid=(n // tc,),
            in_specs=[pl.BlockSpec((tc, d), lambda i, dref: (i, 0))],
            out_specs=pl.BlockSpec(memory_space=pl.ANY),
            scratch_shapes=[pltpu.SemaphoreType.DMA]),
        compiler_params=pltpu.CompilerParams(
            dimension_semantics=("arbitrary",), vmem_limit_bytes=VMEM_LIMIT),
        name="dispatch_rows",
    )(dest, src)


PAIR = 2 * LANES
SUB_ROWS = 128
SPLIT_ROWS = 512


def _even_lane_selector():
    c = np.arange(LANES)
    blk = np.zeros((LANES, LANES), np.float32)
    blk[c, c // 2 + LANES // 2 * (c % 2)] = 1.0
    return jnp.asarray(np.kron(np.eye(2, dtype=np.float32), blk), BF16)


def _expert_kernel(be_ref, nv_ref, nb_ref, x_ref, w1_ref, b1_ref, w2_ref, b2_ref, sel_ref, y_ref,
                   xb_ref):
    i, j = pl.program_id(0), pl.program_id(1)
    n_valid = nv_ref[i]
    tm = xb_ref.shape[0]

    @pl.when(i < nb_ref[0])
    def _():
        @pl.when(j == 0)
        def _():
            row = lax.broadcasted_iota(jnp.int32, (tm, 1), 0)
            xb_ref[...] = jnp.where(row < n_valid, x_ref[...], 0.0).astype(BF16)

        def rows_pass(start_row, n_rows):
            align = min(SUB_ROWS, tm)
            halves = 2 if n_rows >= SPLIT_ROWS else 1
            hr = n_rows // halves
            even = lax.broadcasted_iota(jnp.int32, (hr, LANES), 1) % 2 == 0
            w1b, w2b = w1_ref[...].astype(BF16), w2_ref[...].astype(BF16)
            hcats = []
            for p in range(halves):
                row0 = pl.multiple_of(start_row + p * hr, align)
                hcats.append((row0, _dot(xb_ref[pl.ds(row0, hr), :], w1b) + b1_ref[j]))
            for row0, hcat in hcats:
                acts = []
                for g in range(hcat.shape[1] // PAIR):
                    h0, h1 = hcat[:, g * PAIR:g * PAIR + LANES], hcat[:, g * PAIR + LANES:(g + 1) * PAIR]
                    glu = jnp.where(even, h0, pltpu.roll(h1, 1, 1))
                    lin = jnp.where(even, pltpu.roll(h0, LANES - 1, 1), h1)
                    glu = jnp.minimum(glu, SWIGLU_LIMIT)
                    lin = jnp.clip(lin, -SWIGLU_LIMIT, SWIGLU_LIMIT)
                    acts.append((glu * _sigmoid(glu, SWIGLU_ALPHA) * (lin + 1.0)).astype(BF16))
                hid = []
                for g in range(0, len(acts), 2):
                    width = LANES * len(acts[g:g + 2])
                    hid.append(_dot(jnp.concatenate(acts[g:g + 2], axis=1),
                                    sel_ref[0:width, 0:width]).astype(BF16))
                part = _dot(jnp.concatenate(hid, axis=1), w2b)
                rows = pl.ds(row0, hr)
                start = jnp.where(j == 0, jnp.broadcast_to(b2_ref[...], part.shape), y_ref[rows, :])
                y_ref[rows, :] = start + part

        unit = min(SUB_ROWS, tm)
        n_units = (n_valid + unit - 1) // unit
        for bit in reversed(range((tm // unit).bit_length())):
            @pl.when(((n_units >> bit) & 1) == 1)
            def _(bit=bit):
                rows_pass((n_units >> (bit + 1) << (bit + 1)) * unit, unit << bit)


def _experts(xs, block_expert, block_valid, n_blocks_used, w1, b1, w2, b2, tm, tf=256):
    m_pad, d = xs.shape
    f = w2.shape[1]
    tf = min(tf, f)
    nf = f // tf
    nblk = m_pad // tm
    sel = _even_lane_selector()
    b1 = b1.reshape(b1.shape[0], nf, 1, 2 * tf)

    def blk(i, nb):
        return jnp.minimum(i, nb[0] - 1)

    def ftile(i, j, nb):
        return jnp.where(i < nb[0], j, nf - 1)

    return pl.pallas_call(
        _expert_kernel,
        out_shape=jax.ShapeDtypeStruct((m_pad, d), F32),
        grid_spec=pltpu.PrefetchScalarGridSpec(
            num_scalar_prefetch=3, grid=(nblk, nf),
            in_specs=[
                pl.BlockSpec((tm, d), lambda i, j, be, nv, nb: (blk(i, nb), 0)),
                pl.BlockSpec((None, d, 2 * tf), lambda i, j, be, nv, nb: (be[blk(i, nb)], 0, ftile(i, j, nb))),
                pl.BlockSpec((None, nf, 1, 2 * tf), lambda i, j, be, nv, nb: (be[blk(i, nb)], 0, 0, 0)),
                pl.BlockSpec((None, tf, d), lambda i, j, be, nv, nb: (be[blk(i, nb)], ftile(i, j, nb), 0)),
                pl.BlockSpec((None, 1, d), lambda i, j, be, nv, nb: (be[blk(i, nb)], 0, 0)),
                pl.BlockSpec(sel.shape, lambda i, j, be, nv, nb: (0, 0)),
            ],
            out_specs=pl.BlockSpec((tm, d), lambda i, j, be, nv, nb: (blk(i, nb), 0)),
            scratch_shapes=[pltpu.VMEM((tm, d), BF16)]),
        compiler_params=pltpu.CompilerParams(
            dimension_semantics=("arbitrary", "arbitrary"), vmem_limit_bytes=VMEM_LIMIT),
        name="experts",
    )(block_expert, block_valid, n_blocks_used, xs, w1, b1, w2, b2, sel)


def _combine_kernel(dest_ref, h_ref, gate_ref, nw_ref, y_ref, o_ref, buf_ref, sem, *, tc):
    i = pl.program_id(0)
    slot = i % 2

    def fetch(block, into):
        base = block * tc * TOP_K

        def issue(tile, carry):
            tile_rows = pl.ds(pl.multiple_of(tile * SUBLANES, SUBLANES), SUBLANES)
            for k in range(TOP_K):
                dst_tile = buf_ref.at[into, k, tile_rows]
                for u in range(SUBLANES):
                    pair = base + (tile * SUBLANES + u) * TOP_K + k
                    pltpu.make_async_copy(y_ref.at[pl.ds(dest_ref[pair], 1)], dst_tile.at[pl.ds(u, 1)],
                                          sem.at[into]).start(priority=k % 2)
            return carry

        lax.fori_loop(0, tc // SUBLANES, issue, 0)

    @pl.when(i == 0)
    def _():
        fetch(0, 0)

    @pl.when(i + 1 < pl.num_programs(0))
    def _():
        fetch(i + 1, 1 - slot)

    for k in range(TOP_K):
        pltpu.make_async_copy(y_ref.at[pl.ds(0, tc)], buf_ref.at[slot, k], sem.at[slot]).wait()
    gate = gate_ref[...]
    out = h_ref[...]
    for k in range(TOP_K):
        out = out + gate[:, k:k + 1] * buf_ref[slot, k]
    out = out * lax.rsqrt(jnp.mean(out * out, axis=-1, keepdims=True) + EPS) * nw_ref[...]
    o_ref[...] = out


def _combine(dest, h, gates, norm_w, y, tc=512):
    n, d = h.shape
    tc = min(tc, n)
    return pl.pallas_call(
        functools.partial(_combine_kernel, tc=tc),
        out_shape=jax.ShapeDtypeStruct((n, d), F32),
        grid_spec=pltpu.PrefetchScalarGridSpec(
            num_scalar_prefetch=1, grid=(n // tc,),
            in_specs=[
                pl.BlockSpec((tc, d), lambda i, dref: (i, 0)),
                pl.BlockSpec((tc, TOP_K), lambda i, dref: (i, 0)),
                pl.BlockSpec((1, d), lambda i, dref: (0, 0)),
                pl.BlockSpec(memory_space=pl.ANY),
            ],
            out_specs=pl.BlockSpec((tc, d), lambda i, dref: (i, 0)),
            scratch_shapes=[pltpu.VMEM((2, TOP_K, tc, d), F32), pltpu.SemaphoreType.DMA((2,))]),
        compiler_params=pltpu.CompilerParams(
            dimension_semantics=("arbitrary",), vmem_limit_bytes=VMEM_LIMIT),
        name="combine_norm",
    )(dest, h, gates, norm_w.reshape(1, d), y)


def _moe(h, hn, gates, idx, w1, b1, w2, b2, norm_final_w, tm=1024, tf=256):
    n, d = h.shape
    n_pairs = n * TOP_K
    e_flat = idx.reshape(n_pairs)
    rank, counts = _pair_ranks(idx, blk=min(512, n))
    counts = counts.reshape(N_EXPERTS)
    padded = (counts + tm - 1) // tm * tm
    end_pad = jnp.cumsum(padded)
    start_pad = end_pad - padded
    dest = start_pad[e_flat] + rank.reshape(n_pairs)
    dest = dest.astype(jnp.int32)
    m_pad = -(-(n_pairs + N_EXPERTS * tm) // tm) * tm
    n_blocks = m_pad // tm
    block_start = jnp.arange(n_blocks, dtype=jnp.int32) * tm
    block_expert = jnp.minimum(jnp.sum(end_pad[None, :] <= block_start[:, None], axis=1),
                               N_EXPERTS - 1).astype(jnp.int32)
    block_valid = jnp.clip((start_pad + counts)[block_expert] - block_start, 0, tm).astype(jnp.int32)
    n_blocks_used = (end_pad[-1:] // tm).astype(jnp.int32)
    xs = _dispatch_rows(hn, dest, m_pad, tc=min(1024, n))
    y = _experts(xs, block_expert, block_valid, n_blocks_used, w1, b1, w2, b2[:, None, :], tm, tf)
    return _combine(dest, h, gates, norm_final_w, y)


def kernel(x, w_in, ret_norm_w, hg_norm_w, hg_lb_logits, w_out, norm_mix_w, norm_moe_w, w_router, b_router,
           w1, b1, w2, b2, norm_final_w):
    batch, seq, d = x.shape
    depth = w_in.shape[0]
    assert depth == 1, "the final norm is fused into the last layer's combine; one layer supported"
    lb_all = jnp.cumsum(jax.nn.softmax(hg_lb_logits.astype(F32), axis=0), axis=0)
    x2 = x.reshape(batch * seq, d)
    z = _inproj(x2, norm_mix_w[0], w_in[0])
    o_r = _retention(z, ret_norm_w[0], batch, seq)
    o_h = _hgrn2(z, lb_all[0], hg_norm_w[0], batch, seq)
    h, hn, gates, idx = _outproj(o_r, o_h, x2, w_out[0].astype(BF16), norm_moe_w[0], w_router[0], b_router[0])
    out = _moe(h, hn, gates, idx, w1[0], b1[0], w2[0], b2[0], norm_final_w)
    return out.reshape(batch, seq, d)
```

```python
import functools

import numpy as np
import jax
import jax.numpy as jnp
from jax import lax
from jax.experimental import pallas as pl
from jax.experimental.pallas import tpu as pltpu

F32 = jnp.float32
BF16 = jnp.bfloat16

LANES = 128
SUBLANES = 8
CHUNK = 64
HG_LEVELS = (32, 16, 8, 4, 2, 1)
RET_HEADS = 8
HG_HEADS = 16
N_SPLITS = 10
N_EXPERTS = 32
TOP_K = 4
SWIGLU_ALPHA = 1.702
SWIGLU_LIMIT = 7.0
ROPE_BASE = 10000.0
EPS = 1e-6
LOG2_E = 1.4426950408889634
VMEM_LIMIT = 56 * 1024 * 1024

P_RQ, P_RK, P_RV, P_RG, P_HQ, P_HF, P_HI, P_HG, P_GR, P_GH = range(N_SPLITS)


def _sigmoid(x, scale=1.0):
    return 1.0 / (1.0 + jnp.exp2(x * (-scale * LOG2_E)))


def _sigmoid_pair(x):
    t = jnp.exp(-jnp.abs(x))
    r = 1.0 / (1.0 + t)
    p = t * r
    pos = x >= 0.0
    return jnp.where(pos, r, p), jnp.where(pos, p, r)


def _silu(x):
    return x * _sigmoid(x)


def _dot(a, b):
    return jnp.dot(a, b, preferred_element_type=F32)


def _dot_nt(a, b):
    return lax.dot_general(a, b, (((1,), (1,)), ((), ())), preferred_element_type=F32)


def _dot_tn(a, b):
    return lax.dot_general(a, b, (((0,), (0,)), ((), ())), preferred_element_type=F32)


def _inproj_kernel(x_ref, nw_ref, w_ref, o_ref, xn_ref):
    @pl.when(pl.program_id(1) == 0)
    def _():
        x = x_ref[...]
        ms = jnp.mean(x * x, axis=-1, keepdims=True)
        xn_ref[...] = (x * lax.rsqrt(ms + EPS) * nw_ref[...]).astype(BF16)

    acc = _dot(xn_ref[...], w_ref[...].astype(BF16))
    for g in range(o_ref.shape[0]):
        o_ref[g] = acc[:, g * LANES:(g + 1) * LANES].astype(o_ref.dtype)


def _inproj(x2, norm_w, w_bf, tm=1024, tn=1024):
    n, d = x2.shape
    cols = w_bf.shape[1]
    gpt = tn // LANES
    return pl.pallas_call(
        _inproj_kernel,
        out_shape=jax.ShapeDtypeStruct((cols // LANES, n, LANES), BF16),
        grid=(n // tm, cols // tn),
        in_specs=[
            pl.BlockSpec((tm, d), lambda i, j: (i, 0)),
            pl.BlockSpec((1, d), lambda i, j: (0, 0)),
            pl.BlockSpec((d, tn), lambda i, j: (0, j)),
        ],
        out_specs=pl.BlockSpec((gpt, tm, LANES), lambda i, j: (j, i, 0)),
        scratch_shapes=[pltpu.VMEM((tm, d), BF16)],
        compiler_params=pltpu.CompilerParams(
            dimension_semantics=("parallel", "arbitrary"), vmem_limit_bytes=VMEM_LIMIT),
        name="inproj",
    )(x2, norm_w.reshape(1, d), w_bf)


def _retention_tables(seq, t_blk, dk):
    h = RET_HEADS
    log_gamma = jnp.log1p(-jnp.exp2(-5.0 - jnp.arange(h, dtype=F32)))
    inv = 1.0 / (ROPE_BASE ** jnp.linspace(0.0, 1.0, dk // 2, dtype=F32))
    ang = jnp.arange(seq, dtype=F32)[:, None] * inv[None, :]
    cos, sin = jnp.cos(ang), jnp.sin(ang)
    pos = np.arange(t_blk)
    cq, ck = pos[:, None] // CHUNK, pos[None, :] // CHUNK
    dist = (pos[:, None] - pos[None, :]).astype(np.float32)
    expo = np.where(cq == ck, np.abs(dist), dist)
    allowed = jnp.asarray(cq >= ck)
    dmask = jnp.where(allowed[None], jnp.exp(jnp.asarray(expo)[None] * log_gamma[:, None, None]), 0.0)
    posf = jnp.arange(t_blk, dtype=F32)
    qdec = jnp.exp((posf[None, :] + 1.0) * log_gamma[:, None])[..., None]
    kdec = jnp.exp((t_blk - 1.0 - posf)[None, :] * log_gamma[:, None])[..., None]
    sdec = jnp.broadcast_to(jnp.exp(t_blk * log_gamma)[:, None, None], (h, 1, LANES))
    return cos, sin, dmask, qdec, kdec, sdec


def _retention_kernel(q_ref, k_ref, v_ref, g_ref, gm_ref, cos_ref, sin_ref, dm_ref, qd_ref, kd_ref,
                      sd_ref, nw_ref, o_ref, st_ref, *, scale):
    @pl.when(pl.program_id(2) == 0)
    def _():
        st_ref[...] = jnp.zeros_like(st_ref)

    cos, sin = cos_ref[...], sin_ref[...]
    heads = st_ref.shape[0]
    gph = q_ref.shape[0] // heads
    dk = gph * LANES

    def slabs(ref, hd):
        return [ref[hd * gph + g] for g in range(gph)]

    def rot(ref, hd):
        x1, x2 = [s.astype(F32) for s in slabs(ref, hd)]
        return jnp.concatenate([x1 * cos - x2 * sin, x2 * cos + x1 * sin], axis=-1)

    def stage_rotate(hd):
        q = rot(q_ref, hd)
        k = rot(k_ref, hd) * scale
        return (q.astype(BF16), k.astype(BF16), (q * qd_ref[hd]).astype(BF16), (k * kd_ref[hd]).astype(BF16))

    def stage_matmuls(hd, rotated):
        q_bf, k_bf, q_dec, k_dec = rotated
        v = jnp.concatenate(slabs(v_ref, hd), axis=-1)
        scores = _dot_nt(q_bf, k_bf) * dm_ref[hd]
        state = st_ref[hd]
        o = _dot(scores.astype(BF16), v) + _dot(q_dec, state.astype(BF16))
        st_ref[hd] = sd_ref[hd][:, 0:1] * state + _dot_tn(k_dec, v)
        return o

    def stage_norm(hd, o):
        o = o - jnp.mean(o, axis=-1, keepdims=True)
        o = o * lax.rsqrt(jnp.mean(o * o, axis=-1, keepdims=True) + EPS)
        gate = jnp.concatenate(slabs(g_ref, hd), axis=-1).astype(F32)
        merge = jnp.concatenate(slabs(gm_ref, hd), axis=-1).astype(F32)
        res = _sigmoid(merge) * (o * nw_ref[:, hd * dk:(hd + 1) * dk] * _silu(gate))
        for g in range(gph):
            o_ref[hd * gph + g] = res[:, g * LANES:(g + 1) * LANES].astype(o_ref.dtype)

    rotated = [stage_rotate(hd) for hd in range(heads)]
    outs = [stage_matmuls(hd, rotated[hd]) for hd in range(heads)]
    for hd in range(heads):
        stage_norm(hd, outs[hd])


def _retention(z, ret_norm_w, batch, seq, t_blk=512, heads=4):
    n = batch * seq
    d = ret_norm_w.shape[0]
    dk = d // RET_HEADS
    gps = heads * dk // LANES
    nt = seq // t_blk
    cos, sin, dmask, qdec, kdec, sdec = _retention_tables(seq, t_blk, dk)

    def part(p):
        return pl.BlockSpec((gps, t_blk, LANES), lambda b, h, t, p=p: (p * RET_HEADS // heads + h, b * nt + t, 0))

    per_head = lambda shape: pl.BlockSpec((heads,) + shape, lambda b, h, t: (h, 0, 0))
    return pl.pallas_call(
        functools.partial(_retention_kernel, scale=dk ** -0.5),
        out_shape=jax.ShapeDtypeStruct((d // LANES, n, LANES), BF16),
        grid=(batch, RET_HEADS // heads, nt),
        in_specs=[
            part(P_RQ), part(P_RK), part(P_RV), part(P_RG), part(P_GR),
            pl.BlockSpec((t_blk, dk // 2), lambda b, h, t: (t, 0)),
            pl.BlockSpec((t_blk, dk // 2), lambda b, h, t: (t, 0)),
            per_head((t_blk, t_blk)), per_head((t_blk, 1)), per_head((t_blk, 1)), per_head((1, LANES)),
            pl.BlockSpec((1, heads * dk), lambda b, h, t: (0, h)),
        ],
        out_specs=pl.BlockSpec((gps, t_blk, LANES), lambda b, h, t: (h, b * nt + t, 0)),
        scratch_shapes=[pltpu.VMEM((heads, dk, dk), F32)],
        compiler_params=pltpu.CompilerParams(
            dimension_semantics=("parallel", "parallel", "arbitrary"), vmem_limit_bytes=VMEM_LIMIT),
        name="retention",
    )(z, z, z, z, z, cos, sin, dmask, qdec, kdec, sdec, ret_norm_w.reshape(1, d))


def _hgrn_level_tables():
    c = CHUNK
    r = np.arange(c)
    mats = [(r[None, :] <= r[:, None])]
    masks = [np.eye(c, dtype=bool)]
    for s in HG_LEVELS:
        blk, within = r // (2 * s), r % (2 * s)
        mid = blk * 2 * s + s - 1
        right = within >= s
        t = r[None, :]
        if s < SUBLANES:
            mats.append(np.where(right[:, None], (t > mid[:, None]) & (t <= r[:, None]),
                                 (t > r[:, None]) & (t <= mid[:, None])))
        masks.append((blk[:, None] == blk[None, :]) & right[:, None] & (~right)[None, :])
    wmat = np.concatenate(mats, axis=0).astype(np.float32)
    if len(masks) % 2:
        masks.append(np.zeros((c, c), bool))
    mask = np.stack([np.concatenate(masks[p:p + 2], axis=1) for p in range(0, len(masks), 2)])
    return jnp.asarray(wmat, BF16), jnp.asarray(mask.astype(np.float32), F32)


def _split2(x):
    hi = x.astype(BF16)
    return hi, (x - hi.astype(F32)).astype(BF16)


def _level_exponent(cum, s):
    pieces = []
    for r0 in range(0, CHUNK, SUBLANES):
        mid = r0 // (2 * s) * 2 * s + s - 1
        rows, ref = cum[r0:r0 + SUBLANES], cum[mid:mid + 1]
        pieces.append(rows - ref if r0 % (2 * s) >= s else ref - rows)
    return jnp.concatenate(pieces, axis=0)


def _hgrn_kernel(q_ref, f_ref, i_ref, g_ref, gm_ref, lb_ref, nw_ref, w_ref, m_ref, o_ref,
                 st_ref, qs_ref, ks_ref, ob_ref, qe_ref, kd_ref, dec_ref, el_ref, s_ref, kv_ref):
    @pl.when(pl.program_id(2) == 0)
    def _():
        st_ref[...] = jnp.zeros_like(st_ref)

    c = CHUNK
    t_blk, dk = qs_ref.shape
    n_lvl = len(HG_LEVELS)
    n_chunks = t_blk // c
    lb = lb_ref[...]
    zeros = jnp.zeros((c, dk), BF16)
    state = [st_ref[...]]

    def stage_gates(ci):
        rows = slice(ci * c, (ci + 1) * c)
        z = f_ref[rows, :].astype(F32)
        qs = _silu(q_ref[rows, :].astype(F32))
        sig_pos, sig_neg = _sigmoid_pair(z)
        ks = (1.0 - lb) * sig_neg
        qs_ref[rows, :] = qs.astype(BF16)
        ks_ref[rows, :] = ks.astype(BF16)
        hi, lo = _split2(jnp.log(lb + (1.0 - lb) * sig_pos) * LOG2_E)
        g2 = _dot(w_ref[...], jnp.concatenate([hi, lo], axis=-1))
        g = g2[:, :dk] + g2[:, dk:]
        cum = g[0:c]
        e_start = jnp.exp2(cum)
        qe_ref[rows, :] = (qs * e_start).astype(BF16)
        kd_ref[rows, :] = (ks * jnp.exp2(cum[c - 1:c] - cum)).astype(BF16)
        dec_ref[ci] = e_start[c - 1:c]
        narrow = 1
        for l, s in enumerate(HG_LEVELS):
            if s >= SUBLANES:
                expo = _level_exponent(cum, s)
            else:
                expo = g[narrow * c:(narrow + 1) * c]
                narrow += 1
            el_ref[(ci * n_lvl + l) * c:(ci * n_lvl + l + 1) * c, :] = jnp.exp2(expo).astype(BF16)

    def stage_scores(ci):
        rows = slice(ci * c, (ci + 1) * c)
        qb, kb = qs_ref[rows, :], ks_ref[rows, :]
        lhs, rhs = [qb], [kb]
        for l in range(n_lvl):
            el = el_ref[(ci * n_lvl + l) * c:(ci * n_lvl + l + 1) * c, :]
            lhs.append(qb * el)
            rhs.append(kb * el)
        s = None
        for p in range(m_ref.shape[0]):
            a0, b0 = lhs[2 * p], rhs[2 * p]
            if 2 * p + 1 < len(lhs):
                a = jnp.concatenate([a0, lhs[2 * p + 1]], axis=1)
                b = jnp.concatenate([jnp.concatenate([b0, zeros], axis=1),
                                     jnp.concatenate([zeros, rhs[2 * p + 1]], axis=1)], axis=0)
            else:
                a, b = a0, jnp.concatenate([b0, zeros], axis=0)
            tile = m_ref[p] * _dot_nt(a, b)
            s = tile if s is None else s + tile
        s_ref[rows, :] = s.astype(BF16)

    def stage_intra(ci):
        rows = slice(ci * c, (ci + 1) * c)
        vc = i_ref[rows, :]
        ob_ref[rows, :] = _dot(s_ref[rows, :], jnp.concatenate([vc, vc], axis=0))
        kv_ref[ci] = _dot_tn(vc, kd_ref[rows, :])

    def stage_state(ci):
        rows = slice(ci * c, (ci + 1) * c)
        st = state[0]
        o = ob_ref[rows, :] + _dot_nt(qe_ref[rows, :], st.astype(BF16))
        state[0] = st * dec_ref[ci] + kv_ref[ci]
        o = o * lax.rsqrt(jnp.mean(o * o, axis=-1, keepdims=True) + EPS)
        o = o * nw_ref[...] * _silu(g_ref[rows, :].astype(F32))
        o_ref[rows, :] = (_sigmoid(gm_ref[rows, :].astype(F32)) * o).astype(o_ref.dtype)

    stages = (stage_gates, stage_scores, stage_intra, stage_state)
    for step in range(n_chunks + len(stages) - 1):
        for depth, stage in enumerate(stages):
            if 0 <= step - depth < n_chunks:
                stage(step - depth)
    st_ref[...] = state[0]


def _hgrn2(z, lb, hg_norm_w, batch, seq, t_blk=4096):
    n = batch * seq
    d = hg_norm_w.shape[0]
    dk = d // HG_HEADS
    t_blk = min(t_blk, seq)
    nt = seq // t_blk
    wmat, mask = _hgrn_level_tables()

    def part(p):
        return pl.BlockSpec((None, t_blk, LANES), lambda b, h, t, p=p: (p * HG_HEADS + h, b * nt + t, 0))

    head_vec = pl.BlockSpec((1, dk), lambda b, h, t: (0, h))
    return pl.pallas_call(
        _hgrn_kernel,
        out_shape=jax.ShapeDtypeStruct((d // LANES, n, LANES), BF16),
        grid=(batch, HG_HEADS, nt),
        in_specs=[
            part(P_HQ), part(P_HF), part(P_HI), part(P_HG), part(P_GH), head_vec, head_vec,
            pl.BlockSpec(wmat.shape, lambda b, h, t: (0, 0)),
            pl.BlockSpec(mask.shape, lambda b, h, t: (0, 0, 0)),
        ],
        out_specs=pl.BlockSpec((None, t_blk, LANES), lambda b, h, t: (h, b * nt + t, 0)),
        scratch_shapes=[
            pltpu.VMEM((dk, dk), F32), pltpu.VMEM((t_blk, dk), BF16), pltpu.VMEM((t_blk, dk), BF16),
            pltpu.VMEM((t_blk, dk), F32),
            pltpu.VMEM((t_blk, dk), BF16), pltpu.VMEM((t_blk, dk), BF16),
            pltpu.VMEM((t_blk // CHUNK, 1, dk), F32),
            pltpu.VMEM((t_blk * len(HG_LEVELS), dk), BF16), pltpu.VMEM((t_blk, 2 * CHUNK), BF16),
            pltpu.VMEM((t_blk // CHUNK, dk, dk), F32),
        ],
        compiler_params=pltpu.CompilerParams(
            dimension_semantics=("parallel", "parallel", "arbitrary"), vmem_limit_bytes=VMEM_LIMIT),
        name="hgrn2",
    )(z, z, z, z, z, lb.reshape(1, d), hg_norm_w.reshape(1, d), wmat, mask)


def _outproj_kernel(a_ref, b_ref, x_ref, w_ref, nw_ref, wr_ref, br_ref, h_ref, hn_ref, gate_ref, idx_ref):
    merged = jnp.concatenate([(a_ref[g].astype(F32) + b_ref[g].astype(F32)).astype(BF16)
                              for g in range(a_ref.shape[0])], axis=1)
    h = x_ref[...] + _dot(merged, w_ref[...])
    h_ref[...] = h
    hn = h * lax.rsqrt(jnp.mean(h * h, axis=-1, keepdims=True) + EPS) * nw_ref[...]
    hn_ref[...] = hn
    n_e = br_ref.shape[-1]
    hn_hi, hn_lo = _split2(hn)
    l2 = _dot(hn_hi, wr_ref[...])
    logits = l2[:, :n_e] + l2[:, n_e:] + _dot(hn_lo, wr_ref[:, :n_e]) + br_ref[...]
    lane = lax.broadcasted_iota(jnp.int32, logits.shape, 1).astype(F32)
    vals, idxs = [], []
    for _ in range(TOP_K):
        m = jnp.max(logits, axis=-1, keepdims=True)
        idx = jnp.min(jnp.where(logits == m, lane, float(n_e)), axis=-1, keepdims=True)
        vals.append(m)
        idxs.append(idx)
        logits = jnp.where(lane == idx, -jnp.inf, logits)
    exps = [jnp.exp(v - vals[0]) for v in vals]
    denom = exps[0] + exps[1] + exps[2] + exps[3]
    for k in range(TOP_K):
        gate_ref[:, k:k + 1] = exps[k] / denom
        idx_ref[:, k:k + 1] = idxs[k].astype(jnp.int32)


def _outproj(o_r, o_h, x2, w_bf, norm_w, w_router, b_router, tm=512):
    n, d = x2.shape
    n_e = w_router.shape[1]
    row = lambda w: pl.BlockSpec((tm, w), lambda i: (i, 0))
    full = lambda a: pl.BlockSpec(a.shape, lambda i: (0, 0))
    slabs = pl.BlockSpec((d // LANES, tm, LANES), lambda i: (0, i, 0))
    nw, br = norm_w.reshape(1, d), b_router.reshape(1, n_e)
    wr = jnp.concatenate(_split2(w_router), axis=1)
    return pl.pallas_call(
        _outproj_kernel,
        out_shape=(jax.ShapeDtypeStruct((n, d), F32), jax.ShapeDtypeStruct((n, d), F32),
                   jax.ShapeDtypeStruct((n, TOP_K), F32), jax.ShapeDtypeStruct((n, TOP_K), jnp.int32)),
        grid=(n // tm,),
        in_specs=[slabs, slabs, row(d), full(w_bf), full(nw), full(wr), full(br)],
        out_specs=(row(d), row(d), row(TOP_K), row(TOP_K)),
        compiler_params=pltpu.CompilerParams(
            dimension_semantics=("parallel",), vmem_limit_bytes=VMEM_LIMIT),
        name="outproj_router",
    )(o_r, o_h, x2, w_bf, nw, wr, br)


def _rank_kernel(idx_ref, tri_ref, rank_ref, cnt_ref, run_ref):
    @pl.when(pl.program_id(0) == 0)
    def _():
        run_ref[...] = jnp.zeros_like(run_ref)

    idx = idx_ref[...]
    lane = lax.broadcasted_iota(jnp.int32, (idx.shape[0], N_EXPERTS), 1)
    run = run_ref[...]
    for k in range(TOP_K):
        onehot = lane == idx[:, k:k + 1]
        before = _dot(tri_ref[...], onehot.astype(BF16)) + run
        rank_ref[:, k:k + 1] = jnp.sum(jnp.where(onehot, before, 0.0), axis=-1, keepdims=True).astype(jnp.int32)
        run = run + jnp.sum(onehot.astype(F32), axis=0, keepdims=True)
    run_ref[...] = run
    cnt_ref[...] = run.astype(jnp.int32)


def _pair_ranks(idx, blk=512):
    n = idx.shape[0]
    r = np.arange(blk)
    tri = jnp.asarray(r[None, :] < r[:, None], BF16)
    return pl.pallas_call(
        _rank_kernel,
        out_shape=(jax.ShapeDtypeStruct((n, TOP_K), jnp.int32), jax.ShapeDtypeStruct((1, N_EXPERTS), jnp.int32)),
        grid=(n // blk,),
        in_specs=[pl.BlockSpec((blk, TOP_K), lambda i: (i, 0)), pl.BlockSpec((blk, blk), lambda i: (0, 0))],
        out_specs=(pl.BlockSpec((blk, TOP_K), lambda i: (i, 0)), pl.BlockSpec((1, N_EXPERTS), lambda i: (0, 0))),
        scratch_shapes=[pltpu.VMEM((1, N_EXPERTS), F32)],
        compiler_params=pltpu.CompilerParams(dimension_semantics=("arbitrary",)),
        name="pair_ranks",
    )(idx, tri)


def _dispatch_kernel(dest_ref, src_ref, dst_ref, sem, *, tc):
    base = pl.program_id(0) * tc * TOP_K

    def issue(tile, carry):
        src_tile = src_ref.at[pl.ds(pl.multiple_of(tile * SUBLANES, SUBLANES), SUBLANES)]
        for u in range(SUBLANES):
            for k in range(TOP_K):
                pair = base + (tile * SUBLANES + u) * TOP_K + k
                pltpu.make_async_copy(src_tile.at[pl.ds(u, 1)], dst_ref.at[pl.ds(dest_ref[pair], 1)],
                                      sem).start(priority=k % 2)
        return carry

    lax.fori_loop(0, tc // SUBLANES, issue, 0)
    for _ in range(TOP_K):
        pltpu.make_async_copy(src_ref, dst_ref.at[pl.ds(0, tc)], sem).wait()


def _dispatch_rows(src, dest, m_pad, tc=256):
    n, d = src.shape
    return pl.pallas_call(
        functools.partial(_dispatch_kernel, tc=tc),
        out_shape=jax.ShapeDtypeStruct((m_pad, d), src.dtype),
        grid_spec=pltpu.PrefetchScalarGridSpec(
            num_scalar_prefetch=1, grid=(n // tc,),
            in_specs=[pl.BlockSpec((tc, d), lambda i, dref: (i, 0))],
            out_specs=pl.BlockSpec(memory_space=pl.ANY),
            scratch_shapes=[pltpu.SemaphoreType.DMA]),
        compiler_params=pltpu.CompilerParams(
            dimension_semantics=("arbitrary",), vmem_limit_bytes=VMEM_LIMIT),
        name="dispatch_rows",
    )(dest, src)


PAIR = 2 * LANES
SUB_ROWS = 128
SPLIT_ROWS = 512


def _even_lane_selector():
    c = np.arange(LANES)
    blk = np.zeros((LANES, LANES), np.float32)
    blk[c, c // 2 + LANES // 2 * (c % 2)] = 1.0
    return jnp.asarray(np.kron(np.eye(2, dtype=np.float32), blk), BF16)


def _expert_kernel(be_ref, nv_ref, nb_ref, x_ref, w1_ref, b1_ref, w2_ref, b2_ref, sel_ref, y_ref,
                   xb_ref):
    i, j = pl.program_id(0), pl.program_id(1)
    n_valid = nv_ref[i]
    tm = xb_ref.shape[0]

    @pl.when(i < nb_ref[0])
    def _():
        @pl.when(j == 0)
        def _():
            row = lax.broadcasted_iota(jnp.int32, (tm, 1), 0)
            xb_ref[...] = jnp.where(row < n_valid, x_ref[...], 0.0).astype(BF16)

        def rows_pass(start_row, n_rows):
            align = min(SUB_ROWS, tm)
            halves = 2 if n_rows >= SPLIT_ROWS else 1
            hr = n_rows // halves
            even = lax.broadcasted_iota(jnp.int32, (hr, LANES), 1) % 2 == 0
            w1b, w2b = w1_ref[...].astype(BF16), w2_ref[...].astype(BF16)
            hcats = []
            for p in range(halves):
                row0 = pl.multiple_of(start_row + p * hr, align)
                hcats.append((row0, _dot(xb_ref[pl.ds(row0, hr), :], w1b) + b1_ref[j]))
            for row0, hcat in hcats:
                acts = []
                for g in range(hcat.shape[1] // PAIR):
                    h0, h1 = hcat[:, g * PAIR:g * PAIR + LANES], hcat[:, g * PAIR + LANES:(g + 1) * PAIR]
                    glu = jnp.where(even, h0, pltpu.roll(h1, 1, 1))
                    lin = jnp.where(even, pltpu.roll(h0, LANES - 1, 1), h1)
                    glu = jnp.minimum(glu, SWIGLU_LIMIT)
                    lin = jnp.clip(lin, -SWIGLU_LIMIT, SWIGLU_LIMIT)
                    acts.append((glu * _sigmoid(glu, SWIGLU_ALPHA) * (lin + 1.0)).astype(BF16))
                hid = []
                for g in range(0, len(acts), 2):
                    width = LANES * len(acts[g:g + 2])
                    hid.append(_dot(jnp.concatenate(acts[g:g + 2], axis=1),
                                    sel_ref[0:width, 0:width]).astype(BF16))
                part = _dot(jnp.concatenate(hid, axis=1), w2b)
                rows = pl.ds(row0, hr)
                start = jnp.where(j == 0, jnp.broadcast_to(b2_ref[...], part.shape), y_ref[rows, :])
                y_ref[rows, :] = start + part

        unit = min(SUB_ROWS, tm)
        n_units = (n_valid + unit - 1) // unit
        for bit in reversed(range((tm // unit).bit_length())):
            @pl.when(((n_units >> bit) & 1) == 1)
            def _(bit=bit):
                rows_pass((n_units >> (bit + 1) << (bit + 1)) * unit, unit << bit)


def _experts(xs, block_expert, block_valid, n_blocks_used, w1, b1, w2, b2, tm, tf=256):
    m_pad, d = xs.shape
    f = w2.shape[1]
    tf = min(tf, f)
    nf = f // tf
    nblk = m_pad // tm
    sel = _even_lane_selector()
    b1 = b1.reshape(b1.shape[0], nf, 1, 2 * tf)

    def blk(i, nb):
        return jnp.minimum(i, nb[0] - 1)

    def ftile(i, j, nb):
        return jnp.where(i < nb[0], j, nf - 1)

    return pl.pallas_call(
        _expert_kernel,
        out_shape=jax.ShapeDtypeStruct((m_pad, d), F32),
        grid_spec=pltpu.PrefetchScalarGridSpec(
            num_scalar_prefetch=3, grid=(nblk, nf),
            in_specs=[
                pl.BlockSpec((tm, d), lambda i, j, be, nv, nb: (blk(i, nb), 0)),
                pl.BlockSpec((None, d, 2 * tf), lambda i, j, be, nv, nb: (be[blk(i, nb)], 0, ftile(i, j, nb))),
                pl.BlockSpec((None, nf, 1, 2 * tf), lambda i, j, be, nv, nb: (be[blk(i, nb)], 0, 0, 0)),
                pl.BlockSpec((None, tf, d), lambda i, j, be, nv, nb: (be[blk(i, nb)], ftile(i, j, nb), 0)),
                pl.BlockSpec((None, 1, d), lambda i, j, be, nv, nb: (be[blk(i, nb)], 0, 0)),
                pl.BlockSpec(sel.shape, lambda i, j, be, nv, nb: (0, 0)),
            ],
            out_specs=pl.BlockSpec((tm, d), lambda i, j, be, nv, nb: (blk(i, nb), 0)),
            scratch_shapes=[pltpu.VMEM((tm, d), BF16)]),
        compiler_params=pltpu.CompilerParams(
            dimension_semantics=("arbitrary", "arbitrary"), vmem_limit_bytes=VMEM_LIMIT),
        name="experts",
    )(block_expert, block_valid, n_blocks_used, xs, w1, b1, w2, b2, sel)


def _combine_kernel(dest_ref, h_ref, gate_ref, nw_ref, y_ref, o_ref, buf_ref, sem, *, tc):
    i = pl.program_id(0)
    slot = i % 2

    def fetch(block, into):
        base = block * tc * TOP_K

        def issue(tile, carry):
            tile_rows = pl.ds(pl.multiple_of(tile * SUBLANES, SUBLANES), SUBLANES)
            for k in range(TOP_K):
                dst_tile = buf_ref.at[into, k, tile_rows]
                for u in range(SUBLANES):
                    pair = base + (tile * SUBLANES + u) * TOP_K + k
                    pltpu.make_async_copy(y_ref.at[pl.ds(dest_ref[pair], 1)], dst_tile.at[pl.ds(u, 1)],
                                          sem.at[into]).start(priority=k % 2)
            return carry

        lax.fori_loop(0, tc // SUBLANES, issue, 0)

    @pl.when(i == 0)
    def _():
        fetch(0, 0)

    @pl.when(i + 1 < pl.num_programs(0))
    def _():
        fetch(i + 1, 1 - slot)

    for k in range(TOP_K):
        pltpu.make_async_copy(y_ref.at[pl.ds(0, tc)], buf_ref.at[slot, k], sem.at[slot]).wait()
    gate = gate_ref[...]
    out = h_ref[...]
    for k in range(TOP_K):
        out = out + gate[:, k:k + 1] * buf_ref[slot, k]
    out = out * lax.rsqrt(jnp.mean(out * out, axis=-1, keepdims=True) + EPS) * nw_ref[...]
    o_ref[...] = out


def _combine(dest, h, gates, norm_w, y, tc=256):
    n, d = h.shape
    tc = min(tc, n)
    return pl.pallas_call(
        functools.partial(_combine_kernel, tc=tc),
        out_shape=jax.ShapeDtypeStruct((n, d), F32),
        grid_spec=pltpu.PrefetchScalarGridSpec(
            num_scalar_prefetch=1, grid=(n // tc,),
            in_specs=[
                pl.BlockSpec((tc, d), lambda i, dref: (i, 0)),
                pl.BlockSpec((tc, TOP_K), lambda i, dref: (i, 0)),
                pl.BlockSpec((1, d), lambda i, dref: (0, 0)),
                pl.BlockSpec(memory_space=pl.ANY),
            ],
            out_specs=pl.BlockSpec((tc, d), lambda i, dref: (i, 0)),
            scratch_shapes=[pltpu.VMEM((2, TOP_K, tc, d), F32), pltpu.SemaphoreType.DMA((2,))]),
        compiler_params=pltpu.CompilerParams(
            dimension_semantics=("arbitrary",), vmem_limit_bytes=VMEM_LIMIT),
        name="combine_norm",
    )(dest, h, gates, norm_w.reshape(1, d), y)


def _moe(h, hn, gates, idx, w1, b1, w2, b2, norm_final_w, tm=1024, tf=256):
    n, d = h.shape
    n_pairs = n * TOP_K
    e_flat = idx.reshape(n_pairs)
    rank, counts = _pair_ranks(idx, blk=min(512, n))
    counts = counts.reshape(N_EXPERTS)
    padded = (counts + tm - 1) // tm * tm
    end_pad = jnp.cumsum(padded)
    start_pad = end_pad - padded
    dest = start_pad[e_flat] + rank.reshape(n_pairs)
    dest = dest.astype(jnp.int32)
    m_pad = -(-(n_pairs + N_EXPERTS * tm) // tm) * tm
    n_blocks = m_pad // tm
    block_start = jnp.arange(n_blocks, dtype=jnp.int32) * tm
    block_expert = jnp.minimum(jnp.sum(end_pad[None, :] <= block_start[:, None], axis=1),
                               N_EXPERTS - 1).astype(jnp.int32)
    block_valid = jnp.clip((start_pad + counts)[block_expert] - block_start, 0, tm).astype(jnp.int32)
    n_blocks_used = (end_pad[-1:] // tm).astype(jnp.int32)
    xs = _dispatch_rows(hn, dest, m_pad, tc=min(2048, n))
    y = _experts(xs, block_expert, block_valid, n_blocks_used, w1, b1, w2, b2[:, None, :], tm, tf)
    return _combine(dest, h, gates, norm_final_w, y)


def kernel(x, w_in, ret_norm_w, hg_norm_w, hg_lb_logits, w_out, norm_mix_w, norm_moe_w, w_router, b_router,
           w1, b1, w2, b2, norm_final_w):
    batch, seq, d = x.shape
    depth = w_in.shape[0]
    assert depth == 1, "the final norm is fused into the last layer's combine; one layer supported"
    lb_all = jnp.cumsum(jax.nn.softmax(hg_lb_logits.astype(F32), axis=0), axis=0)
    x2 = x.reshape(batch * seq, d)
    z = _inproj(x2, norm_mix_w[0], w_in[0])
    o_r = _retention(z, ret_norm_w[0], batch, seq)
    o_h = _hgrn2(z, lb_all[0], hg_norm_w[0], batch, seq)
    h, hn, gates, idx = _outproj(o_r, o_h, x2, w_out[0].astype(BF16), norm_moe_w[0], w_router[0], b_router[0])
    out = _moe(h, hn, gates, idx, w1[0], b1[0], w2[0], b2[0], norm_final_w)
    return out.reshape(batch, seq, d)
```
